```python
import jax, jax.numpy as jnp
from jax import lax
import numpy as np

D_MODEL = 1024
BATCH = 1
SEQ = 16384
DEPTH = 2
DEC_BATCH = 32
DEC_SEQ = 4
PAST_LEN = 16384
PAGE_SIZE = 128

CONV_WIDTH = D_MODEL // 2
CONV_K = 3
N_HEADS_SB = 8
HEAD_DIM = (D_MODEL // 2) // N_HEADS_SB
ATT_WIDTH = N_HEADS_SB * HEAD_DIM
Q_BLOCK = 128
SB_BIAS_INIT = -8.0
SPLITS = (CONV_WIDTH, 2 * CONV_WIDTH, 3 * CONV_WIDTH,
          3 * CONV_WIDTH + ATT_WIDTH, 3 * CONV_WIDTH + 2 * ATT_WIDTH)
IN_AB_WIDTH = 3 * CONV_WIDTH + 3 * ATT_WIDTH
MIX_AB_WIDTH = CONV_WIDTH + ATT_WIDTH
POOL_WIDTH = D_MODEL
POOL_WINDOWS = (2, 4, 8, 16)
N_POOL_GROUPS = len(POOL_WINDOWS)
POOL_GROUP_DIM = POOL_WIDTH // N_POOL_GROUPS
POOL_BUF = max(POOL_WINDOWS) - 1
N_EXPERTS = 16
N_EXPERT_GROUPS = 4
EXPERTS_PER_GROUP = N_EXPERTS // N_EXPERT_GROUPS
TOP_K = 2
D_FF_EXPERT = 2 * D_MODEL
N_EVEN = (DEPTH + 1) // 2
N_ODD = DEPTH // 2
DEEPNORM_ALPHA = (2 * DEPTH) ** 0.25
DEEPNORM_BETA = (8 * DEPTH) ** -0.25
LN_EPS = 1e-5

kernel_name = "hybrid_conv_stickbreaking_pool_moe_step"


def layer_norm(x, g, b):
    xf = x.astype(jnp.float32)
    mu = jnp.mean(xf, axis=-1, keepdims=True)
    var = jnp.mean(jnp.square(xf - mu), axis=-1, keepdims=True)
    y = (xf - mu) * lax.rsqrt(var + LN_EPS) * g.astype(jnp.float32) + b.astype(jnp.float32)
    return y.astype(x.dtype)


def post_norm(x, sub, g, b):
    return layer_norm(DEEPNORM_ALPHA * x + sub, g, b)


def short_conv(h, buf, w):
    L = h.shape[1]
    ext = jnp.concatenate([buf.astype(h.dtype), h], axis=1)
    y = w[0] * ext[:, 0:L] + w[1] * ext[:, 1:L + 1] + w[2] * ext[:, 2:L + 2]
    return y, ext[:, -(CONV_K - 1):]


def stick_breaking_block(qb, qpos, k, v, kpos, bias):
    z = jnp.einsum('bqhd,bkhd->bhqk', qb.astype(jnp.float32), k.astype(jnp.float32)) * (HEAD_DIM ** -0.5)
    z = z + bias.astype(jnp.float32)[None, :, None, None]
    mask = (kpos[None, :] < qpos[:, None])[None, None]
    log1m = jnp.where(mask, -jax.nn.softplus(z), 0.0)
    log_a = jax.nn.log_sigmoid(z) + lax.cumsum(log1m, axis=3, reverse=True) - log1m
    a = jnp.where(mask, jnp.exp(log_a), 0.0)
    return jnp.einsum('bhqk,bkhd->bqhd', a, v.astype(jnp.float32)).astype(v.dtype)


def stick_breaking_prompt(q, k, v, bias):
    B, S, H, Dh = q.shape
    n_blocks = S // Q_BLOCK
    kpos = jnp.arange(S)

    def one_block(i):
        qb = lax.dynamic_slice_in_dim(q, i * Q_BLOCK, Q_BLOCK, axis=1)
        qpos = i * Q_BLOCK + jnp.arange(Q_BLOCK)
        return stick_breaking_block(qb, qpos, k, v, kpos, bias)

    out = lax.map(one_block, jnp.arange(n_blocks))
    return jnp.transpose(out, (1, 0, 2, 3, 4)).reshape(B, S, H, Dh)


def even_mixer(x, conv_buf, k_past, v_past, past_len, w_in, w_conv, bias, w_out):
    B, L, _ = x.shape
    proj = x @ w_in
    gb, gc, xv, q, k, v = jnp.split(proj, SPLITS, axis=-1)
    yc, new_buf = short_conv(gc * xv, conv_buf, w_conv)
    yc = gb * yc
    q = q.reshape(B, L, N_HEADS_SB, HEAD_DIM)
    k = k.reshape(B, L, N_HEADS_SB, HEAD_DIM)
    v = v.reshape(B, L, N_HEADS_SB, HEAD_DIM)
    if k_past is None:
        att = stick_breaking_prompt(q, k, v, bias)
    else:
        k_all = jnp.concatenate([k_past.astype(k.dtype), k], axis=1)
        v_all = jnp.concatenate([v_past.astype(v.dtype), v], axis=1)
        att = stick_breaking_block(q, past_len + jnp.arange(L), k_all, v_all, jnp.arange(past_len + L), bias)
    y = jnp.concatenate([yc, att.reshape(B, L, ATT_WIDTH)], axis=-1) @ w_out
    return y, new_buf, k, v


def multiscale_pool(u, buf, start_pos):
    L = u.shape[1]
    ext_raw = jnp.concatenate([buf.astype(u.dtype), u], axis=1)
    ext = ext_raw.astype(jnp.float32)
    cs = jnp.concatenate([jnp.zeros_like(ext[:, :1]), jnp.cumsum(ext, axis=1)], axis=1)
    pos = start_pos + jnp.arange(L)
    hi = cs[:, POOL_BUF + 1:]
    outs = []
    for g, w in enumerate(POOL_WINDOWS):
        c0, c1 = g * POOL_GROUP_DIM, (g + 1) * POOL_GROUP_DIM
        lo = cs[:, POOL_BUF + 1 - w:POOL_BUF + 1 - w + L, c0:c1]
        cnt = jnp.minimum(pos + 1, w).astype(jnp.float32)[None, :, None]
        outs.append((hi[..., c0:c1] - lo) / cnt)
    pooled = jnp.concatenate(outs, axis=-1)
    mixed = (pooled - ext[:, POOL_BUF:]).astype(u.dtype)
    return mixed, ext_raw[:, -POOL_BUF:]


def odd_mixer(x, pool_buf, start_pos, w_in, w_grp, scale, w_out):
    B, L, _ = x.shape
    u = x @ w_in
    d, new_buf = multiscale_pool(u, pool_buf, start_pos)
    d = d.reshape(B, L, N_POOL_GROUPS, POOL_GROUP_DIM)
    y = jnp.einsum('blgc,gcd->blgd', d, w_grp).reshape(B, L, POOL_WIDTH) * scale
    return y @ w_out, new_buf


def grouped_moe(x, w_router, router_bias, w_up, w_down):
    B, L, D = x.shape
    xt = x.reshape(B * L, D)
    s = jax.nn.sigmoid((xt @ w_router).astype(jnp.float32))
    sel = s + router_bias.astype(jnp.float32)
    sel_g = sel.reshape(-1, N_EXPERT_GROUPS, EXPERTS_PER_GROUP)
    group_score = jnp.sum(lax.top_k(sel_g, TOP_K)[0], axis=-1)
    top_group = jnp.argmax(group_score, axis=-1)
    gmask = jax.nn.one_hot(top_group, N_EXPERT_GROUPS, dtype=jnp.bool_)[:, :, None]
    masked = jnp.where(gmask, sel_g, -jnp.inf).reshape(-1, N_EXPERTS)
    _, idx = lax.top_k(masked, TOP_K)
    gs = jnp.take_along_axis(s, idx, axis=1)
    gates = gs / jnp.sum(gs, axis=-1, keepdims=True)
    combine = jnp.sum(jax.nn.one_hot(idx, N_EXPERTS, dtype=jnp.float32) * gates[..., None], axis=1).astype(x.dtype)
    y = jnp.zeros_like(xt)
    for e in range(N_EXPERTS):
        h = jax.nn.gelu(xt @ w_up[e])
        y = y + combine[:, e:e + 1] * (h @ w_down[e])
    return y.reshape(B, L, D)


def setup_inputs(seed: int = 0) -> dict:
    key = jax.random.key(seed)
    ks = jax.random.split(key, 24)
    n_pages = PAST_LEN // PAGE_SIZE
    n_used = DEC_BATCH * n_pages
    n_phys = n_used + n_used // 4

    def nrm(k, shape, scale):
        return scale * jax.random.normal(k, shape, jnp.float32)

    page_table = jax.random.permutation(ks[0], n_phys)[:n_used].reshape(DEC_BATCH, n_pages).astype(jnp.int32)
    return {
        "x_prompt": nrm(ks[1], (BATCH, SEQ, D_MODEL), 1.0),
        "x_sample": nrm(ks[2], (DEC_BATCH, DEC_SEQ, D_MODEL), 1.0),
        "cache_k": nrm(ks[3], (N_EVEN, n_phys, PAGE_SIZE, N_HEADS_SB, HEAD_DIM), 1.0),
        "cache_v": nrm(ks[4], (N_EVEN, n_phys, PAGE_SIZE, N_HEADS_SB, HEAD_DIM), 1.0),
        "state_conv": nrm(ks[5], (N_EVEN, DEC_BATCH, CONV_K - 1, CONV_WIDTH), 1.0),
        "state_pool": nrm(ks[6], (N_ODD, DEC_BATCH, POOL_BUF, POOL_WIDTH), 1.0),
        "page_table": page_table,
        "w_in_ab": nrm(ks[7], (N_EVEN, D_MODEL, IN_AB_WIDTH), D_MODEL ** -0.5),
        "conv_w": nrm(ks[8], (N_EVEN, CONV_K, CONV_WIDTH), CONV_K ** -0.5),
        "sb_bias": SB_BIAS_INIT + nrm(ks[22], (N_EVEN, N_HEADS_SB), 0.5),
        "w_out_ab": nrm(ks[9], (N_EVEN, MIX_AB_WIDTH, D_MODEL), DEEPNORM_BETA * MIX_AB_WIDTH ** -0.5),
        "w_in_c": nrm(ks[10], (N_ODD, D_MODEL, POOL_WIDTH), D_MODEL ** -0.5),
        "w_grp_c": nrm(ks[11], (N_ODD, N_POOL_GROUPS, POOL_GROUP_DIM, POOL_GROUP_DIM), POOL_GROUP_DIM ** -0.5),
        "scale_c": 1.0 + nrm(ks[12], (N_ODD, POOL_WIDTH), 0.1),
        "w_out_c": nrm(ks[13], (N_ODD, POOL_WIDTH, D_MODEL), DEEPNORM_BETA * POOL_WIDTH ** -0.5),
        "w_router": nrm(ks[14], (D_MODEL, N_EXPERTS), D_MODEL ** -0.5),
        "router_bias": nrm(ks[15], (N_EXPERTS,), 0.01),
        "w_exp_up": nrm(ks[16], (DEPTH, N_EXPERTS, D_MODEL, D_FF_EXPERT), D_MODEL ** -0.5),
        "w_exp_down": nrm(ks[17], (DEPTH, N_EXPERTS, D_FF_EXPERT, D_MODEL), DEEPNORM_BETA * D_FF_EXPERT ** -0.5),
        "ln_mix_g": 1.0 + nrm(ks[18], (DEPTH, D_MODEL), 0.05),
        "ln_mix_b": nrm(ks[19], (DEPTH, D_MODEL), 0.02),
        "ln_ffn_g": 1.0 + nrm(ks[20], (DEPTH, D_MODEL), 0.05),
        "ln_ffn_b": nrm(ks[21], (DEPTH, D_MODEL), 0.02),
    }


def reference(x_prompt, x_sample, cache_k, cache_v, state_conv, state_pool, page_table,
              w_in_ab, conv_w, sb_bias, w_out_ab, w_in_c, w_grp_c, scale_c, w_out_c,
              w_router, router_bias, w_exp_up, w_exp_down,
              ln_mix_g, ln_mix_b, ln_ffn_g, ln_ffn_b):
    n_dec, n_pages = page_table.shape
    past_len = n_pages * PAGE_SIZE
    xp, xs = x_prompt, x_sample
    kp_l, vp_l, ks_l, vs_l, cp_l, cs_l, pp_l, ps_l = [], [], [], [], [], [], [], []
    for layer in range(DEPTH):
        if layer % 2 == 0:
            e = layer // 2
            zero_buf = jnp.zeros((xp.shape[0], CONV_K - 1, CONV_WIDTH), xp.dtype)
            mp, cbuf_p, kp, vp = even_mixer(xp, zero_buf, None, None, 0,
                                            w_in_ab[e], conv_w[e], sb_bias[e], w_out_ab[e])
            k_past = cache_k[e][page_table].reshape(n_dec, past_len, N_HEADS_SB, HEAD_DIM)
            v_past = cache_v[e][page_table].reshape(n_dec, past_len, N_HEADS_SB, HEAD_DIM)
            ms, cbuf_s, ks_new, vs_new = even_mixer(xs, state_conv[e], k_past, v_past, past_len,
                                                    w_in_ab[e], conv_w[e], sb_bias[e], w_out_ab[e])
            kp_l.append(kp); vp_l.append(vp); ks_l.append(ks_new); vs_l.append(vs_new)
            cp_l.append(cbuf_p); cs_l.append(cbuf_s)
        else:
            o = layer // 2
            zero_pool = jnp.zeros((xp.shape[0], POOL_BUF, POOL_WIDTH), xp.dtype)
            mp, pbuf_p = odd_mixer(xp, zero_pool, 0, w_in_c[o], w_grp_c[o], scale_c[o], w_out_c[o])
            ms, pbuf_s = odd_mixer(xs, state_pool[o], past_len, w_in_c[o], w_grp_c[o], scale_c[o], w_out_c[o])
            pp_l.append(pbuf_p); ps_l.append(pbuf_s)
        xp = post_norm(xp, mp, ln_mix_g[layer], ln_mix_b[layer])
        xs = post_norm(xs, ms, ln_mix_g[layer], ln_mix_b[layer])
        xp = post_norm(xp, grouped_moe(xp, w_router, router_bias, w_exp_up[layer], w_exp_down[layer]),
                       ln_ffn_g[layer], ln_ffn_b[layer])
        xs = post_norm(xs, grouped_moe(xs, w_router, router_bias, w_exp_up[layer], w_exp_down[layer]),
                       ln_ffn_g[layer], ln_ffn_b[layer])
    return (xp, xs,
            jnp.stack(kp_l), jnp.stack(vp_l), jnp.stack(ks_l), jnp.stack(vs_l),
            jnp.stack(cp_l), jnp.stack(cs_l), jnp.stack(pp_l), jnp.stack(ps_l))
```

```python
import functools
import math

import jax
import jax.numpy as jnp
from jax import lax
from jax.experimental import pallas as pl
from jax.experimental.pallas import tpu as pltpu

V7X_LANES = 128
V7X_SUBLANES = 8
V7X_VMEM_BYTES = 64 * 1024 * 1024
VMEM_LIMIT = V7X_VMEM_BYTES - 8 * 1024 * 1024

PAGE = 128
HEAD_DIM = 64
HEADS_PER_BLOCK = V7X_LANES // HEAD_DIM
POOL_WINDOWS = (2, 4, 8, 16)
POOL_PAD = 16
LN_EPS = 1e-5
N_GROUPS = 4
GROUP_SIZE = 4

BF16 = jnp.bfloat16
F32 = jnp.float32


def _params(*sem):
    return pltpu.CompilerParams(dimension_semantics=sem, vmem_limit_bytes=VMEM_LIMIT)


def _dot(a, b):
    return jnp.dot(a, b, preferred_element_type=F32)


def _dot_nt(a, b):
    return lax.dot_general(a, b, (((1,), (1,)), ((), ())), preferred_element_type=F32)


def _softplus(z):
    return jnp.maximum(z, 0.0) + jnp.log1p(jnp.exp(-jnp.abs(z)))


def _split_bf16(x):
    hi = x.astype(BF16)
    lo = (x - hi.astype(F32)).astype(BF16)
    return hi, lo


def _post_ln(x, sub, g, b, alpha):
    y = alpha * x + sub
    mu = jnp.mean(y, axis=-1, keepdims=True)
    yc = y - mu
    var = jnp.mean(yc * yc, axis=-1, keepdims=True)
    return yc * lax.rsqrt(var + LN_EPS) * g + b


def _proj_even_kernel(*refs, cw, seq_len, has_state):
    if has_state:
        (x_ref, w_ref, cwt_ref, p1_ref, p2_ref,
         yc_ref, h_ref, q_ref, k_ref, v_ref, kb_ref, vb_ref, carry_ref) = refs
    else:
        (x_ref, w_ref, cwt_ref,
         yc_ref, h_ref, q_ref, k_ref, v_ref, kb_ref, vb_ref, carry_ref) = refs
    i = pl.program_id(0)
    tm = x_ref.shape[0]
    xb = x_ref[...].astype(BF16)

    def sec(n):
        return _dot(xb, w_ref[:, n * cw:(n + 1) * cw])

    gb, gc, xv = sec(0), sec(1), sec(2)
    h = gc * xv
    hm1 = pltpu.roll(h, 1, axis=0)
    hm2 = pltpu.roll(h, 2, axis=0)
    row = lax.broadcasted_iota(jnp.int32, (tm, 1), 0)
    if has_state:
        l = row % seq_len
        hm1 = jnp.where(l < 1, p1_ref[...], hm1)
        hm2 = jnp.where(l < 2, p2_ref[...], hm2)
    else:
        @pl.when(i == 0)
        def _():
            carry_ref[...] = jnp.zeros_like(carry_ref)
        prev = carry_ref[...]
        last = V7X_SUBLANES - 1
        hm1 = jnp.where(row == 0, prev[last:last + 1, :], hm1)
        hm2 = jnp.where(row == 0, prev[last - 1:last, :], hm2)
        hm2 = jnp.where(row == 1, prev[last:last + 1, :], hm2)
        carry_ref[...] = h[tm - V7X_SUBLANES:, :]
    cwt = cwt_ref[...]
    conv = cwt[0:1, :] * hm2 + cwt[1:2, :] * hm1 + cwt[2:3, :] * h
    yc_ref[...] = (gb * conv).astype(BF16)
    h_ref[...] = h
    q_ref[...] = (sec(3) * (HEAD_DIM ** -0.5)).astype(BF16)
    k = sec(4)
    v = sec(5)
    k_ref[...] = k
    v_ref[...] = v
    kb_ref[...] = k.astype(BF16)
    vb_ref[...] = v.astype(BF16)


def proj_even(x, w_bf, conv_w, state_rows, seq_len, tm):
    T, D = x.shape
    cw = conv_w.shape[1]
    assert w_bf.shape == (D, 6 * cw) and T % tm == 0
    has_state = state_rows is not None
    if has_state:
        assert T == tm
    row_in = pl.BlockSpec((tm, D), lambda i: (i, 0))
    row_out = pl.BlockSpec((tm, cw), lambda i: (i, 0))
    in_specs = [row_in,
                pl.BlockSpec((D, 6 * cw), lambda i: (0, 0)),
                pl.BlockSpec((3, cw), lambda i: (0, 0))]
    args = [x, w_bf, conv_w]
    if has_state:
        in_specs += [row_out, row_out]
        args += list(state_rows)
    out_shape = [jax.ShapeDtypeStruct((T, cw), d) for d in (BF16, F32, BF16, F32, F32, BF16, BF16)]
    return pl.pallas_call(
        functools.partial(_proj_even_kernel, cw=cw, seq_len=seq_len, has_state=has_state),
        grid=(T // tm,),
        in_specs=in_specs,
        out_specs=[row_out] * 7,
        out_shape=out_shape,
        scratch_shapes=[pltpu.VMEM((V7X_SUBLANES, cw), F32)],
        compiler_params=_params("arbitrary"),
        name="proj_even",
    )(*args)


def _sb_chunk(qm, ks, vs, nu, bias_h, r, mask):
    z = _dot_nt(qm, ks) + bias_h
    sp = _softplus(z)
    spm = sp if mask is None else jnp.where(mask, sp, 0.0)
    hi, lo = _split_bf16(spm)
    c = _dot(hi, nu) + _dot(lo, nu)
    a = jnp.exp((z - sp) + c + r)
    if mask is not None:
        a = jnp.where(mask, a, 0.0)
    pv = _dot(a.astype(BF16), vs)
    return pv, r - jnp.sum(spm, axis=1, keepdims=True)


def _sb_prompt_kernel(bias_ref, q_ref, k_ref, v_ref, nu_ref, o_ref, acc_ref, r_ref, *, tk):
    p = pl.program_id(0)
    i = pl.program_id(1)
    tq = q_ref.shape[0]
    nd = tq // tk
    lane = lax.broadcasted_iota(jnp.int32, (1, V7X_LANES), 1)
    nu = nu_ref[...]
    q = q_ref[...]
    outs = []
    for hh in range(HEADS_PER_BLOCK):
        hmask = (lane // HEAD_DIM) == hh
        qm = jnp.where(hmask, q, jnp.zeros_like(q))
        bias_h = bias_ref[HEADS_PER_BLOCK * p + hh]
        acc_ref[...] = jnp.zeros_like(acc_ref)
        r_ref[...] = jnp.zeros_like(r_ref)
        qpos = i * tq + lax.broadcasted_iota(jnp.int32, (tq, 1), 0)
        for d in reversed(range(nd)):
            start = pl.multiple_of((i * nd + d) * tk, tk)
            kpos = start + lax.broadcasted_iota(jnp.int32, (1, tk), 1)
            pv, r = _sb_chunk(qm, k_ref[pl.ds(start, tk), :], v_ref[pl.ds(start, tk), :],
                              nu, bias_h, r_ref[...], kpos < qpos)
            acc_ref[...] += pv
            r_ref[...] = r

        def body(jj, carry):
            start = pl.multiple_of((i * nd - 1 - jj) * tk, tk)
            pv, r = _sb_chunk(qm, k_ref[pl.ds(start, tk), :], v_ref[pl.ds(start, tk), :],
                              nu, bias_h, r_ref[...], None)
            acc_ref[...] += pv
            r_ref[...] = r
            return carry

        lax.fori_loop(0, i * nd, body, 0)
        outs.append((hmask, acc_ref[...]))
    o_ref[...] = jnp.where(outs[0][0], outs[0][1], outs[1][1]).astype(o_ref.dtype)


def _neg_upper(n):
    j = lax.broadcasted_iota(jnp.int32, (n, n), 0)
    s = lax.broadcasted_iota(jnp.int32, (n, n), 1)
    return jnp.where(j > s, -1.0, 0.0).astype(BF16)


def sb_prompt(q_bf, k_bf, v_bf, bias, tq, tk):
    T, W = q_bf.shape
    assert T % tq == 0 and tq % tk == 0 and W % V7X_LANES == 0
    nblk = W // V7X_LANES
    qspec = pl.BlockSpec((tq, V7X_LANES), lambda p, i: (i, p))
    kvspec = pl.BlockSpec((T, V7X_LANES), lambda p, i: (0, p))
    return pl.pallas_call(
        functools.partial(_sb_prompt_kernel, tk=tk),
        grid=(nblk, T // tq),
        in_specs=[pl.BlockSpec(memory_space=pltpu.SMEM), qspec, kvspec, kvspec,
                  pl.BlockSpec((tk, tk), lambda p, i: (0, 0))],
        out_specs=qspec,
        out_shape=jax.ShapeDtypeStruct((T, W), BF16),
        scratch_shapes=[pltpu.VMEM((tq, V7X_LANES), F32), pltpu.VMEM((tq, 1), F32)],
        compiler_params=_params("arbitrary", "arbitrary"),
        name="sb_prompt",
    )(bias, q_bf, k_bf, v_bf, _neg_upper(tk))


def _sb_decode_kernel(pt_ref, qbd_ref, brow_ref, knew_ref, vnew_ref, nu_ref, *rest, pages, n_new, n_heads):
    k_refs = rest[:pages]
    v_refs = rest[pages:2 * pages]
    o_ref, acc_ref, r_ref = rest[2 * pages:]
    j = pl.program_id(1)
    qbd = qbd_ref[...]
    brow = brow_ref[...]
    nu = nu_ref[...]
    rows = qbd.shape[0]

    @pl.when(j == 0)
    def _():
        l = lax.broadcasted_iota(jnp.int32, (rows, 1), 0) // n_heads
        c = lax.broadcasted_iota(jnp.int32, (1, PAGE), 1)
        pv, r = _sb_chunk(qbd, knew_ref[...], vnew_ref[...], nu, brow,
                          jnp.zeros((rows, 1), F32), c < l)
        acc_ref[...] = pv
        r_ref[...] = r

    for t in reversed(range(pages)):
        pv, r = _sb_chunk(qbd, k_refs[t][...].astype(BF16), v_refs[t][...].astype(BF16),
                          nu, brow, r_ref[...], None)
        acc_ref[...] += pv
        r_ref[...] = r

    @pl.when(j == pl.num_programs(1) - 1)
    def _():
        acc = acc_ref[...]
        rr = lax.broadcasted_iota(jnp.int32, acc.shape, 0) % n_heads
        cc = lax.broadcasted_iota(jnp.int32, acc.shape, 1) // HEAD_DIM
        accm = jnp.where(rr == cc, acc, 0.0)
        o_ref[...] = jnp.sum(accm.reshape(n_new, n_heads, acc.shape[1]), axis=1).astype(o_ref.dtype)


def sb_decode(q_bf, k_new_bf, v_new_bf, cache_k, cache_v, page_table, bias, n_heads, pages):
    B, L, W = q_bf.shape
    n_pages = page_table.shape[1]
    assert n_pages % pages == 0 and W == n_heads * HEAD_DIM and L <= PAGE
    rows = L * n_heads
    col_head = jnp.arange(W, dtype=jnp.int32) // HEAD_DIM
    keep = (jnp.arange(rows, dtype=jnp.int32) % n_heads)[:, None] == col_head[None, :]
    qbd = jnp.where(keep[None], jnp.repeat(q_bf, n_heads, axis=1), jnp.zeros((), BF16))
    brow = jnp.tile(bias.astype(F32), L).reshape(rows, 1)
    pad = ((0, 0), (0, PAGE - L), (0, 0))
    knew = jnp.pad(k_new_bf, pad)
    vnew = jnp.pad(v_new_bf, pad)
    steps = n_pages // pages

    def page_map(t):
        def index(b, j, pt):
            return (pt[b * n_pages + (steps - 1 - j) * pages + t], 0, 0)
        return index

    per_b = lambda shape: pl.BlockSpec((None,) + shape, lambda b, j, pt: (b, 0, 0))
    page_specs = [pl.BlockSpec((None, PAGE, W), page_map(t)) for t in range(pages)]
    grid_spec = pltpu.PrefetchScalarGridSpec(
        num_scalar_prefetch=1,
        grid=(B, steps),
        in_specs=[per_b((rows, W)),
                  pl.BlockSpec((rows, 1), lambda b, j, pt: (0, 0)),
                  per_b((PAGE, W)), per_b((PAGE, W)),
                  pl.BlockSpec((PAGE, PAGE), lambda b, j, pt: (0, 0))] + page_specs + page_specs,
        out_specs=per_b((L, W)),
        scratch_shapes=[pltpu.VMEM((rows, W), F32), pltpu.VMEM((rows, 1), F32)],
    )
    return pl.pallas_call(
        functools.partial(_sb_decode_kernel, pages=pages, n_new=L, n_heads=n_heads),
        grid_spec=grid_spec,
        out_shape=jax.ShapeDtypeStruct((B, L, W), F32),
        compiler_params=_params("arbitrary", "arbitrary"),
        name="sb_decode",
    )(page_table.reshape(-1), qbd, brow, knew, vnew, _neg_upper(PAGE),
      *([cache_k] * pages), *([cache_v] * pages))


def _out_even_kernel(x_ref, yc_ref, att_ref, wt_ref, wb_ref, g_ref, b_ref, o_ref, ob_ref, *, alpha):
    m = _dot(yc_ref[...], wt_ref[...]) + _dot(att_ref[...].astype(BF16), wb_ref[...])
    y = _post_ln(x_ref[...], m, g_ref[...], b_ref[...], alpha)
    o_ref[...] = y
    ob_ref[...] = y.astype(BF16)


def out_even(x, yc, att, w_out_bf, g, b, alpha, tm):
    T, D = x.shape
    cw = yc.shape[1]
    row = lambda w: pl.BlockSpec((tm, w), lambda i: (i, 0))
    full = lambda a: pl.BlockSpec(a.shape, lambda i: (0,) * a.ndim)
    wt, wb = w_out_bf[:cw], w_out_bf[cw:]
    g2, b2 = g.reshape(1, D), b.reshape(1, D)
    return pl.pallas_call(
        functools.partial(_out_even_kernel, alpha=alpha),
        grid=(T // tm,),
        in_specs=[row(D), row(cw), row(att.shape[1]), full(wt), full(wb), full(g2), full(b2)],
        out_specs=[row(D), row(D)],
        out_shape=[jax.ShapeDtypeStruct((T, D), F32), jax.ShapeDtypeStruct((T, D), BF16)],
        compiler_params=_params("parallel"),
        name="out_even",
    )(x, yc, att, wt, wb, g2, b2)


def _router_kernel(x_ref, wrt_ref, rb_ref, o_ref):
    n_exp = wrt_ref.shape[0]
    logits = lax.dot_general(wrt_ref[...], x_ref[...], (((1,), (1,)), ((), ())),
                             precision=lax.Precision.HIGHEST, preferred_element_type=F32)
    s = jax.nn.sigmoid(logits)
    sel = s + rb_ref[...]
    srow = [s[e:e + 1, :] for e in range(n_exp)]
    row = [sel[e:e + 1, :] for e in range(n_exp)]
    best_score, best_g = None, None
    for g in range(N_GROUPS):
        m = row[g * GROUP_SIZE:(g + 1) * GROUP_SIZE]
        score = None
        for a in range(GROUP_SIZE):
            for b in range(a + 1, GROUP_SIZE):
                pair = m[a] + m[b]
                score = pair if score is None else jnp.maximum(score, pair)
        if best_score is None:
            best_score, best_g = score, jnp.zeros_like(score, dtype=jnp.int32)
        else:
            upd = score > best_score
            best_score = jnp.where(upd, score, best_score)
            best_g = jnp.where(upd, g, best_g)
    neg = -jnp.inf
    masked = [jnp.where(best_g == (e // GROUP_SIZE), row[e], neg) for e in range(n_exp)]

    def first_argmax(vals):
        bv, bi = vals[0], jnp.zeros_like(best_g)
        for e in range(1, n_exp):
            upd = vals[e] > bv
            bv = jnp.where(upd, vals[e], bv)
            bi = jnp.where(upd, e, bi)
        return bi

    i1 = first_argmax(masked)
    i2 = first_argmax([jnp.where(i1 == e, neg, masked[e]) for e in range(n_exp)])
    g1 = sum(jnp.where(i1 == e, srow[e], 0.0) for e in range(n_exp))
    g2 = sum(jnp.where(i2 == e, srow[e], 0.0) for e in range(n_exp))
    tot = g1 + g2
    w1, w2 = g1 / tot, g2 / tot
    for e in range(n_exp):
        o_ref[e:e + 1, :] = jnp.where(i1 == e, w1, 0.0) + jnp.where(i2 == e, w2, 0.0)


def router(x, w_router, router_bias, tm):
    T, D = x.shape
    E = w_router.shape[1]
    return pl.pallas_call(
        _router_kernel,
        grid=(T // tm,),
        in_specs=[pl.BlockSpec((tm, D), lambda i: (i, 0)),
                  pl.BlockSpec((E, D), lambda i: (0, 0)),
                  pl.BlockSpec((E, 1), lambda i: (0, 0))],
        out_specs=pl.BlockSpec((E, tm), lambda i: (0, i)),
        out_shape=jax.ShapeDtypeStruct((E, T), F32),
        compiler_params=_params("parallel"),
        name="router",
    )(x, w_router.T, router_bias.reshape(E, 1).astype(F32))


def _gelu_tanh(x):
    c = math.sqrt(2.0 / math.pi)
    return 0.5 * x * (1.0 + jnp.tanh(c * (x + 0.044715 * (x * x * x))))


def _moe_kernel(xb_ref, x_ref, comb_ref, wu_ref, wd_ref, g_ref, b_ref, o_ref, ob_ref, acc_ref, *, alpha):
    e = pl.program_id(1)

    @pl.when(e == 0)
    def _():
        acc_ref[...] = jnp.zeros_like(acc_ref)

    h = _gelu_tanh(_dot(xb_ref[...], wu_ref[...]))
    y = _dot(h.astype(BF16), wd_ref[...])
    comb = comb_ref[...]
    lane = lax.broadcasted_iota(jnp.int32, comb.shape, 1)
    ce = jnp.sum(jnp.where(lane == e, comb, 0.0), axis=1, keepdims=True)
    acc_ref[...] += ce * y

    @pl.when(e == pl.num_programs(1) - 1)
    def _():
        out = _post_ln(x_ref[...], acc_ref[...], g_ref[...], b_ref[...], alpha)
        o_ref[...] = out
        ob_ref[...] = out.astype(BF16)


def moe_dense(x, x_bf, comb, w_up_bf, w_down_bf, g, b, alpha, tm):
    T, D = x.shape
    E, _, F = w_up_bf.shape
    row = lambda w: pl.BlockSpec((tm, w), lambda i, e: (i, 0))
    vec = pl.BlockSpec((1, D), lambda i, e: (0, 0))
    return pl.pallas_call(
        functools.partial(_moe_kernel, alpha=alpha),
        grid=(T // tm, E),
        in_specs=[row(D), row(D), row(E),
                  pl.BlockSpec((None, D, F), lambda i, e: (e, 0, 0)),
                  pl.BlockSpec((None, F, D), lambda i, e: (e, 0, 0)),
                  vec, vec],
        out_specs=[row(D), row(D)],
        out_shape=[jax.ShapeDtypeStruct((T, D), F32), jax.ShapeDtypeStruct((T, D), BF16)],
        scratch_shapes=[pltpu.VMEM((tm, D), F32)],
        compiler_params=_params("parallel", "arbitrary"),
        name="moe",
    )(x_bf, x, comb, w_up_bf, w_down_bf, g.reshape(1, D), b.reshape(1, D))


def _matmul_kernel(x_ref, w_ref, o_ref):
    o_ref[...] = _dot(x_ref[...].astype(BF16), w_ref[...])


def matmul(x, w_bf, tm):
    T, K = x.shape
    N = w_bf.shape[1]
    return pl.pallas_call(
        _matmul_kernel,
        grid=(T // tm,),
        in_specs=[pl.BlockSpec((tm, K), lambda i: (i, 0)), pl.BlockSpec((K, N), lambda i: (0, 0))],
        out_specs=pl.BlockSpec((tm, N), lambda i: (i, 0)),
        out_shape=jax.ShapeDtypeStruct((T, N), F32),
        compiler_params=_params("parallel"),
        name="matmul",
    )(x, w_bf)


def _pool_windows(ext_ref, tm, pos0, o_ref):
    C = ext_ref.shape[1]
    gw = C // len(POOL_WINDOWS)
    pos = pos0 + lax.broadcasted_iota(jnp.int32, (tm, 1), 0)
    for g, w in enumerate(POOL_WINDOWS):
        cols = slice(g * gw, (g + 1) * gw)
        cur = ext_ref[pl.ds(POOL_PAD, tm), cols]
        tot = cur
        for d in range(1, w):
            tot = tot + ext_ref[pl.ds(POOL_PAD - d, tm), cols]
        cnt = jnp.minimum(pos + 1, w).astype(F32)
        o_ref[:, cols] = (tot / cnt - cur).astype(o_ref.dtype)


def _pool_prompt_kernel(u_ref, o_ref, ext_ref):
    i = pl.program_id(0)
    tm = u_ref.shape[0]

    @pl.when(i == 0)
    def _():
        ext_ref[pl.ds(0, POOL_PAD), :] = jnp.zeros((POOL_PAD, ext_ref.shape[1]), F32)

    ext_ref[pl.ds(POOL_PAD, tm), :] = u_ref[...]
    _pool_windows(ext_ref, tm, i * tm, o_ref)
    ext_ref[pl.ds(0, POOL_PAD), :] = ext_ref[pl.ds(tm, POOL_PAD), :]


def pool_prompt(u, tm):
    T, C = u.shape
    spec = pl.BlockSpec((tm, C), lambda i: (i, 0))
    return pl.pallas_call(
        _pool_prompt_kernel,
        grid=(T // tm,),
        in_specs=[spec],
        out_specs=spec,
        out_shape=jax.ShapeDtypeStruct((T, C), BF16),
        scratch_shapes=[pltpu.VMEM((POOL_PAD + tm, C), F32)],
        compiler_params=_params("arbitrary"),
        name="pool_prompt",
    )(u)


def _pool_sample_kernel(ext_ref, o_ref, *, n_new, start_pos):
    _pool_windows(ext_ref, n_new, start_pos, o_ref)


def pool_sample(ext, n_new, start_pos):
    B, R, C = ext.shape
    assert R == POOL_PAD + n_new
    return pl.pallas_call(
        functools.partial(_pool_sample_kernel, n_new=n_new, start_pos=start_pos),
        grid=(B,),
        in_specs=[pl.BlockSpec((None, R, C), lambda b: (b, 0, 0))],
        out_specs=pl.BlockSpec((None, n_new, C), lambda b: (b, 0, 0)),
        out_shape=jax.ShapeDtypeStruct((B, n_new, C), F32),
        compiler_params=_params("parallel"),
        name="pool_sample",
    )(ext)


def _out_odd_kernel(x_ref, d_ref, wg_ref, sc_ref, wo_ref, g_ref, b_ref, o_ref, ob_ref, *, alpha):
    d = d_ref[...].astype(BF16)
    ng, gw, _ = wg_ref.shape
    ys = [_dot(d[:, g * gw:(g + 1) * gw], wg_ref[g]) for g in range(ng)]
    y = (jnp.concatenate(ys, axis=1) * sc_ref[...]).astype(BF16)
    out = _post_ln(x_ref[...], _dot(y, wo_ref[...]), g_ref[...], b_ref[...], alpha)
    o_ref[...] = out
    ob_ref[...] = out.astype(BF16)


def out_odd(x, d, w_grp_bf, scale, w_out_bf, g, b, alpha, tm):
    T, D = x.shape
    row = pl.BlockSpec((tm, D), lambda i: (i, 0))
    full = lambda a: pl.BlockSpec(a.shape, lambda i: (0,) * a.ndim)
    sc, g2, b2 = scale.reshape(1, D), g.reshape(1, D), b.reshape(1, D)
    return pl.pallas_call(
        functools.partial(_out_odd_kernel, alpha=alpha),
        grid=(T // tm,),
        in_specs=[row, row, full(w_grp_bf), full(sc), full(w_out_bf), full(g2), full(b2)],
        out_specs=[row, row],
        out_shape=[jax.ShapeDtypeStruct((T, D), F32), jax.ShapeDtypeStruct((T, D), BF16)],
        compiler_params=_params("parallel"),
        name="out_odd",
    )(x, d, w_grp_bf, sc, w_out_bf, g2, b2)


TM_PROMPT = 512
TQ_PROMPT = 512
TK_PROMPT = 256
DECODE_PAGES = 8
TM_MOE = 512


def _ffn(x, x_bf, w_router, router_bias, w_up_bf, w_down_bf, g, b, alpha, tm_route, tm_moe):
    comb = router(x, w_router, router_bias, tm_route).T
    return moe_dense(x, x_bf, comb, w_up_bf, w_down_bf, g, b, alpha, tm_moe)


def kernel(x_prompt, x_sample, cache_k, cache_v, state_conv, state_pool, page_table, w_in_ab, conv_w, sb_bias, w_out_ab, w_in_c, w_grp_c, scale_c, w_out_c, w_router, router_bias, w_exp_up, w_exp_down, ln_mix_g, ln_mix_b, ln_ffn_g, ln_ffn_b):
    bp, sp_len, D = x_prompt.shape
    bs, ls, _ = x_sample.shape
    assert bp == 1
    depth = w_exp_up.shape[0]
    n_heads = sb_bias.shape[1]
    cw = conv_w.shape[2]
    aw = n_heads * HEAD_DIM
    n_pages = page_table.shape[1]
    past_len = n_pages * PAGE
    alpha = (2 * depth) ** 0.25
    Tp, Ts = bp * sp_len, bs * ls

    xp = x_prompt.reshape(Tp, D)
    xs = x_sample.reshape(Ts, D)
    xp_bf = xs_bf = None
    outs = {n: [] for n in ("kp", "vp", "ks", "vs", "cp", "cs", "pp", "ps")}

    for layer in range(depth):
        w_up_bf = w_exp_up[layer].astype(BF16)
        w_down_bf = w_exp_down[layer].astype(BF16)
        if layer % 2 == 0:
            e = layer // 2
            w_in_bf = w_in_ab[e].astype(BF16)
            w_out_bf = w_out_ab[e].astype(BF16)
            yc, h, q, k, v, kb, vb = proj_even(xp, w_in_bf, conv_w[e], None, sp_len, TM_PROMPT)
            att = sb_prompt(q, kb, vb, sb_bias[e], TQ_PROMPT, TK_PROMPT)
            xp, xp_bf = out_even(xp, yc, att, w_out_bf, ln_mix_g[layer], ln_mix_b[layer], alpha, TM_PROMPT)
            outs["kp"].append(k.reshape(bp, sp_len, n_heads, HEAD_DIM))
            outs["vp"].append(v.reshape(bp, sp_len, n_heads, HEAD_DIM))
            outs["cp"].append(h.reshape(bp, sp_len, cw)[:, sp_len - 2:])
            st = state_conv[e]
            zero = jnp.zeros((bs, 1, cw), F32)
            p1 = jnp.concatenate([st[:, 1:2]] + [zero] * (ls - 1), axis=1).reshape(Ts, cw)
            p2 = jnp.concatenate([st[:, 0:1], st[:, 1:2]] + [zero] * (ls - 2), axis=1).reshape(Ts, cw)
            yc, h, q, k, v, kb, vb = proj_even(xs, w_in_bf, conv_w[e], (p1, p2), ls, Ts)
            att = sb_decode(q.reshape(bs, ls, aw), kb.reshape(bs, ls, aw), vb.reshape(bs, ls, aw),
                            cache_k.reshape(-1, PAGE, aw), cache_v.reshape(-1, PAGE, aw),
                            page_table + e * cache_k.shape[1], sb_bias[e], n_heads,
                            DECODE_PAGES).reshape(Ts, aw)
            xs, xs_bf = out_even(xs, yc, att, w_out_bf, ln_mix_g[layer], ln_mix_b[layer], alpha, Ts)
            outs["ks"].append(k.reshape(bs, ls, n_heads, HEAD_DIM))
            outs["vs"].append(v.reshape(bs, ls, n_heads, HEAD_DIM))
            hs = jnp.concatenate([st, h.reshape(bs, ls, cw)], axis=1)
            outs["cs"].append(hs[:, -2:])
        else:
            o = layer // 2
            w_in_bf = w_in_c[o].astype(BF16)
            w_grp_bf = w_grp_c[o].astype(BF16)
            w_out_bf = w_out_c[o].astype(BF16)
            buf = POOL_PAD - 1
            u = matmul(xp, w_in_bf, TM_PROMPT)
            d = pool_prompt(u, TM_PROMPT)
            xp, xp_bf = out_odd(xp, d, w_grp_bf, scale_c[o], w_out_bf,
                                ln_mix_g[layer], ln_mix_b[layer], alpha, TM_PROMPT)
            up = jnp.concatenate([jnp.zeros((bp, buf, D), F32), u.reshape(bp, sp_len, D)], axis=1)
            outs["pp"].append(up[:, -buf:])
            u = matmul(xs, w_in_bf, Ts)
            ext = jnp.concatenate([jnp.zeros((bs, 1, D), F32), state_pool[o], u.reshape(bs, ls, D)], axis=1)
            d = pool_sample(ext, ls, past_len).reshape(Ts, D)
            xs, xs_bf = out_odd(xs, d, w_grp_bf, scale_c[o], w_out_bf,
                                ln_mix_g[layer], ln_mix_b[layer], alpha, Ts)
            outs["ps"].append(ext[:, -buf:])
        xp, xp_bf = _ffn(xp, xp_bf, w_router, router_bias, w_up_bf, w_down_bf,
                         ln_ffn_g[layer], ln_ffn_b[layer], alpha, TM_PROMPT, TM_MOE)
        xs, xs_bf = _ffn(xs, xs_bf, w_router, router_bias, w_up_bf, w_down_bf,
                         ln_ffn_g[layer], ln_ffn_b[layer], alpha, Ts, Ts)

    st = lambda n: jnp.stack(outs[n])
    return (xp.reshape(bp, sp_len, D), xs.reshape(bs, ls, D),
            st("kp"), st("vp"), st("ks"), st("vs"), st("cp"), st("cs"), st("pp"), st("ps"))
```

```python
import functools
import math

import jax
import jax.numpy as jnp
from jax import lax
from jax.experimental import pallas as pl
from jax.experimental.pallas import tpu as pltpu

V7X_LANES = 128
V7X_SUBLANES = 8
V7X_VMEM_BYTES = 64 * 1024 * 1024
VMEM_LIMIT = V7X_VMEM_BYTES - 8 * 1024 * 1024

PAGE = 128
HEAD_DIM = 64
HEADS_PER_BLOCK = V7X_LANES // HEAD_DIM
POOL_WINDOWS = (2, 4, 8, 16)
POOL_PAD = 16
LN_EPS = 1e-5
N_GROUPS = 4
GROUP_SIZE = 4

BF16 = jnp.bfloat16
F32 = jnp.float32


def _params(*sem):
    return pltpu.CompilerParams(dimension_semantics=sem, vmem_limit_bytes=VMEM_LIMIT)


def _dot(a, b):
    return jnp.dot(a, b, preferred_element_type=F32)


def _dot_nt(a, b):
    return lax.dot_general(a, b, (((1,), (1,)), ((), ())), preferred_element_type=F32)


LOG2E = math.log2(math.e)


SOFTPLUS2_CLAMP = 100.0


def _softplus2(z2):
    return jnp.maximum(z2, jnp.log2(1.0 + jnp.exp2(jnp.minimum(z2, SOFTPLUS2_CLAMP))))


def _post_ln(x, sub, g, b, alpha):
    y = alpha * x + sub
    mu = jnp.mean(y, axis=-1, keepdims=True)
    yc = y - mu
    var = jnp.mean(yc * yc, axis=-1, keepdims=True)
    return yc * lax.rsqrt(var + LN_EPS) * g + b


def _store_head_expanded(o_ref, x):
    lane = lax.broadcasted_iota(jnp.int32, (1, V7X_LANES), 1)
    for h in range(x.shape[1] // HEAD_DIM):
        p, hh = divmod(h, HEADS_PER_BLOCK)
        blk = x[:, p * V7X_LANES:(p + 1) * V7X_LANES]
        o_ref[:, h * V7X_LANES:(h + 1) * V7X_LANES] = jnp.where(
            (lane // HEAD_DIM) == hh, blk, 0.0).astype(o_ref.dtype)


def _proj_even_kernel(*refs, cw, seq_len, has_state, expand_heads):
    if has_state:
        (x_ref, w_ref, cwt_ref, p1_ref, p2_ref,
         yc_ref, h_ref, q_ref, k_ref, v_ref, kb_ref, vb_ref, carry_ref) = refs
    else:
        (x_ref, w_ref, cwt_ref,
         yc_ref, h_ref, q_ref, k_ref, v_ref, kb_ref, vb_ref, carry_ref) = refs
    i = pl.program_id(0)
    tm = x_ref.shape[0]
    xb = x_ref[...].astype(BF16)

    def sec(n):
        return _dot(xb, w_ref[:, n * cw:(n + 1) * cw])

    gb, gc, xv = sec(0), sec(1), sec(2)
    h = gc * xv
    hm1 = pltpu.roll(h, 1, axis=0)
    hm2 = pltpu.roll(h, 2, axis=0)
    row = lax.broadcasted_iota(jnp.int32, (tm, 1), 0)
    if has_state:
        l = row % seq_len
        hm1 = jnp.where(l < 1, p1_ref[...], hm1)
        hm2 = jnp.where(l < 2, p2_ref[...], hm2)
    else:
        @pl.when(i == 0)
        def _():
            carry_ref[...] = jnp.zeros_like(carry_ref)
        prev = carry_ref[...]
        last = V7X_SUBLANES - 1
        hm1 = jnp.where(row == 0, prev[last:last + 1, :], hm1)
        hm2 = jnp.where(row == 0, prev[last - 1:last, :], hm2)
        hm2 = jnp.where(row == 1, prev[last:last + 1, :], hm2)
        carry_ref[...] = h[tm - V7X_SUBLANES:, :]
    cwt = cwt_ref[...]
    conv = cwt[0:1, :] * hm2 + cwt[1:2, :] * hm1 + cwt[2:3, :] * h
    yc_ref[...] = (gb * conv).astype(BF16)
    h_ref[...] = h
    q_ref[...] = (sec(3) * (HEAD_DIM ** -0.5 * LOG2E)).astype(BF16)
    k = sec(4)
    v = sec(5)
    k_ref[...] = k
    v_ref[...] = v
    if expand_heads:
        _store_head_expanded(kb_ref, k)
        _store_head_expanded(vb_ref, v)
    else:
        kb_ref[...] = k.astype(BF16)
        vb_ref[...] = v.astype(BF16)


def proj_even(x, w_bf, conv_w, state_rows, seq_len, tm, expand_heads):
    T, D = x.shape
    cw = conv_w.shape[1]
    assert w_bf.shape == (D, 6 * cw) and T % tm == 0
    has_state = state_rows is not None
    if has_state:
        assert T == tm
    bw = cw * HEADS_PER_BLOCK if expand_heads else cw
    row_in = pl.BlockSpec((tm, D), lambda i: (i, 0))
    row_out = pl.BlockSpec((tm, cw), lambda i: (i, 0))
    row_kv = pl.BlockSpec((tm, bw), lambda i: (i, 0))
    in_specs = [row_in,
                pl.BlockSpec((D, 6 * cw), lambda i: (0, 0)),
                pl.BlockSpec((3, cw), lambda i: (0, 0))]
    args = [x, w_bf, conv_w]
    if has_state:
        in_specs += [row_out, row_out]
        args += list(state_rows)
    out_shape = [jax.ShapeDtypeStruct((T, cw), d) for d in (BF16, F32, BF16, F32, F32)]
    out_shape += [jax.ShapeDtypeStruct((T, bw), BF16)] * 2
    return pl.pallas_call(
        functools.partial(_proj_even_kernel, cw=cw, seq_len=seq_len, has_state=has_state,
                          expand_heads=expand_heads),
        grid=(T // tm,),
        in_specs=in_specs,
        out_specs=[row_out] * 5 + [row_kv] * 2,
        out_shape=out_shape,
        scratch_shapes=[pltpu.VMEM((V7X_SUBLANES, cw), F32)],
        compiler_params=_params("arbitrary"),
        name="proj_even",
    )(*args)


def _sb_chunk(q, kcat, vcat, nu, bias, rs, mask):
    tk = nu.shape[0]
    groups = kcat.shape[0] // tk
    cols = lambda x, g: x[:, g * tk:(g + 1) * tk]
    z = _dot_nt(q, kcat) + bias
    sp = _softplus2(z)
    if mask is not None:
        mask = jnp.concatenate([mask] * groups, axis=1)
        spm = jnp.where(mask, sp, 0.0)
    else:
        spm = sp
    hi = spm.astype(BF16)
    off = [_dot(cols(hi, g), nu) + rs[g] for g in range(groups)]
    a = jnp.exp2((z - sp) + jnp.concatenate(off, axis=1))
    if mask is not None:
        a = jnp.where(mask, a, 0.0)
    pv = _dot(a.astype(BF16), vcat)
    return pv, [rs[g] - jnp.sum(cols(spm, g), axis=1, keepdims=True) for g in range(groups)]


def _sb_prompt_kernel(bias_ref, q_ref, k_ref, v_ref, nu_ref, o_ref, acc_ref, r_ref, *, tk):
    p = pl.program_id(0)
    i = pl.program_id(1)
    tq = q_ref.shape[0]
    nd = tq // tk
    groups = HEADS_PER_BLOCK
    col = lax.broadcasted_iota(jnp.int32, (1, groups * tk), 1)
    bias = jnp.zeros((1, groups * tk), F32)
    for g in range(groups):
        bias = jnp.where(col // tk == g, bias_ref[groups * p + g] * LOG2E, bias)
    acc_ref[...] = jnp.zeros_like(acc_ref)
    r_ref[...] = jnp.zeros_like(r_ref)

    def rows(ref, start):
        return jnp.concatenate([ref[pl.ds(start, tk), g * V7X_LANES:(g + 1) * V7X_LANES]
                                for g in range(groups)], axis=0)

    def step(start, mask):
        pv, rs = _sb_chunk(q_ref[...], rows(k_ref, start), rows(v_ref, start), nu_ref[...], bias,
                           [r_ref[g] for g in range(groups)], mask)
        acc_ref[...] += pv
        for g in range(groups):
            r_ref[g] = rs[g]

    qpos = i * tq + lax.broadcasted_iota(jnp.int32, (tq, 1), 0)
    for d in reversed(range(nd)):
        start = pl.multiple_of((i * nd + d) * tk, tk)
        kpos = start + lax.broadcasted_iota(jnp.int32, (1, tk), 1)
        step(start, kpos < qpos)

    def body(jj, carry):
        step(pl.multiple_of((i * nd - 1 - jj) * tk, tk), None)
        return carry

    lax.fori_loop(0, i * nd, body, 0)
    o_ref[...] = acc_ref[...].astype(o_ref.dtype)


def _neg_upper(n):
    j = lax.broadcasted_iota(jnp.int32, (n, n), 0)
    s = lax.broadcasted_iota(jnp.int32, (n, n), 1)
    return jnp.where(j > s, -1.0, 0.0).astype(BF16)


def sb_prompt(q_bf, kx_bf, vx_bf, bias, tq, tk):
    T, W = q_bf.shape
    assert T % tq == 0 and tq % tk == 0 and W % V7X_LANES == 0
    assert kx_bf.shape == (T, HEADS_PER_BLOCK * W)
    nblk = W // V7X_LANES
    qspec = pl.BlockSpec((tq, V7X_LANES), lambda p, i: (i, p))
    kvspec = pl.BlockSpec((T, HEADS_PER_BLOCK * V7X_LANES), lambda p, i: (0, p))
    return pl.pallas_call(
        functools.partial(_sb_prompt_kernel, tk=tk),
        grid=(nblk, T // tq),
        in_specs=[pl.BlockSpec(memory_space=pltpu.SMEM), qspec, kvspec, kvspec,
                  pl.BlockSpec((tk, tk), lambda p, i: (0, 0))],
        out_specs=qspec,
        out_shape=jax.ShapeDtypeStruct((T, W), BF16),
        scratch_shapes=[pltpu.VMEM((tq, V7X_LANES), F32),
                        pltpu.VMEM((HEADS_PER_BLOCK, tq, 1), F32)],
        compiler_params=_params("arbitrary", "arbitrary"),
        name="sb_prompt",
    )(bias, q_bf, kx_bf, vx_bf, _neg_upper(tk))


def _sb_decode_kernel(pt_ref, qbd_ref, brow_ref, knew_ref, vnew_ref, nu_ref, *rest, pages, n_new, n_heads):
    k_refs = rest[:pages]
    v_refs = rest[pages:2 * pages]
    o_ref, acc_ref, r_ref = rest[2 * pages:]
    j = pl.program_id(1)
    qbd = qbd_ref[...]
    brow = brow_ref[...]
    nu = nu_ref[...]
    rows = qbd.shape[0]

    @pl.when(j == 0)
    def _():
        l = lax.broadcasted_iota(jnp.int32, (rows, 1), 0) // n_heads
        c = lax.broadcasted_iota(jnp.int32, (1, PAGE), 1)
        pv, (r,) = _sb_chunk(qbd, knew_ref[...], vnew_ref[...], nu, brow,
                             [jnp.zeros((rows, 1), F32)], c < l)
        acc_ref[...] = pv
        r_ref[...] = r

    for t in reversed(range(pages)):
        pv, (r,) = _sb_chunk(qbd, k_refs[t][...].astype(BF16), v_refs[t][...].astype(BF16),
                             nu, brow, [r_ref[...]], None)
        acc_ref[...] += pv
        r_ref[...] = r

    @pl.when(j == pl.num_programs(1) - 1)
    def _():
        acc = acc_ref[...]
        rr = lax.broadcasted_iota(jnp.int32, acc.shape, 0) % n_heads
        cc = lax.broadcasted_iota(jnp.int32, acc.shape, 1) // HEAD_DIM
        accm = jnp.where(rr == cc, acc, 0.0)
        o_ref[...] = jnp.sum(accm.reshape(n_new, n_heads, acc.shape[1]), axis=1).astype(o_ref.dtype)


def sb_decode(q_bf, k_new_bf, v_new_bf, cache_k, cache_v, page_table, bias, n_heads, pages):
    B, L, W = q_bf.shape
    n_pages = page_table.shape[1]
    assert n_pages % pages == 0 and W == n_heads * HEAD_DIM and L <= PAGE
    rows = L * n_heads
    col_head = jnp.arange(W, dtype=jnp.int32) // HEAD_DIM
    keep = (jnp.arange(rows, dtype=jnp.int32) % n_heads)[:, None] == col_head[None, :]
    qbd = jnp.where(keep[None], jnp.repeat(q_bf, n_heads, axis=1), jnp.zeros((), BF16))
    brow = jnp.tile(bias.astype(F32) * LOG2E, L).reshape(rows, 1)
    pad = ((0, 0), (0, PAGE - L), (0, 0))
    knew = jnp.pad(k_new_bf, pad)
    vnew = jnp.pad(v_new_bf, pad)
    steps = n_pages // pages

    def page_map(t):
        def index(b, j, pt):
            return (pt[b * n_pages + (steps - 1 - j) * pages + t], 0, 0)
        return index

    per_b = lambda shape: pl.BlockSpec((None,) + shape, lambda b, j, pt: (b, 0, 0))
    page_specs = [pl.BlockSpec((None, PAGE, W), page_map(t)) for t in range(pages)]
    grid_spec = pltpu.PrefetchScalarGridSpec(
        num_scalar_prefetch=1,
        grid=(B, steps),
        in_specs=[per_b((rows, W)),
                  pl.BlockSpec((rows, 1), lambda b, j, pt: (0, 0)),
                  per_b((PAGE, W)), per_b((PAGE, W)),
                  pl.BlockSpec((PAGE, PAGE), lambda b, j, pt: (0, 0))] + page_specs + page_specs,
        out_specs=per_b((L, W)),
        scratch_shapes=[pltpu.VMEM((rows, W), F32), pltpu.VMEM((rows, 1), F32)],
    )
    return pl.pallas_call(
        functools.partial(_sb_decode_kernel, pages=pages, n_new=L, n_heads=n_heads),
        grid_spec=grid_spec,
        out_shape=jax.ShapeDtypeStruct((B, L, W), F32),
        compiler_params=_params("arbitrary", "arbitrary"),
        name="sb_decode",
    )(page_table.reshape(-1), qbd, brow, knew, vnew, _neg_upper(PAGE),
      *([cache_k] * pages), *([cache_v] * pages))


def _out_even_kernel(x_ref, yc_ref, att_ref, wt_ref, wb_ref, g_ref, b_ref, o_ref, *, alpha):
    m = _dot(yc_ref[...], wt_ref[...]) + _dot(att_ref[...].astype(BF16), wb_ref[...])
    o_ref[...] = _post_ln(x_ref[...], m, g_ref[...], b_ref[...], alpha)


def out_even(x, yc, att, w_out_bf, g, b, alpha, tm):
    T, D = x.shape
    cw = yc.shape[1]
    row = lambda w: pl.BlockSpec((tm, w), lambda i: (i, 0))
    full = lambda a: pl.BlockSpec(a.shape, lambda i: (0,) * a.ndim)
    wt, wb = w_out_bf[:cw], w_out_bf[cw:]
    g2, b2 = g.reshape(1, D), b.reshape(1, D)
    return pl.pallas_call(
        functools.partial(_out_even_kernel, alpha=alpha),
        grid=(T // tm,),
        in_specs=[row(D), row(cw), row(att.shape[1]), full(wt), full(wb), full(g2), full(b2)],
        out_specs=row(D),
        out_shape=jax.ShapeDtypeStruct((T, D), F32),
        compiler_params=_params("parallel"),
        name="out_even",
    )(x, yc, att, wt, wb, g2, b2)


def _router_kernel(x_ref, wrt_ref, rb_ref, before_ref, eid_ref, gate_ref, rank_ref, count_ref, cnt_ref):
    n_exp = wrt_ref.shape[0]
    logits = lax.dot_general(wrt_ref[...], x_ref[...], (((1,), (1,)), ((), ())),
                             precision=lax.Precision.HIGHEST, preferred_element_type=F32)
    s = jax.nn.sigmoid(logits)
    sel = s + rb_ref[...]
    srow = [s[e:e + 1, :] for e in range(n_exp)]
    row = [sel[e:e + 1, :] for e in range(n_exp)]
    best_score, best_g = None, None
    for g in range(N_GROUPS):
        m = row[g * GROUP_SIZE:(g + 1) * GROUP_SIZE]
        score = None
        for a in range(GROUP_SIZE):
            for b in range(a + 1, GROUP_SIZE):
                pair = m[a] + m[b]
                score = pair if score is None else jnp.maximum(score, pair)
        if best_score is None:
            best_score, best_g = score, jnp.zeros_like(score, dtype=jnp.int32)
        else:
            upd = score > best_score
            best_score = jnp.where(upd, score, best_score)
            best_g = jnp.where(upd, g, best_g)
    neg = -jnp.inf
    masked = [jnp.where(best_g == (e // GROUP_SIZE), row[e], neg) for e in range(n_exp)]

    def first_argmax(vals):
        bv, bi = vals[0], jnp.zeros_like(best_g)
        for e in range(1, n_exp):
            upd = vals[e] > bv
            bv = jnp.where(upd, vals[e], bv)
            bi = jnp.where(upd, e, bi)
        return bi

    i1 = first_argmax(masked)
    i2 = first_argmax([jnp.where(i1 == e, neg, masked[e]) for e in range(n_exp)])
    g1 = sum(jnp.where(i1 == e, srow[e], 0.0) for e in range(n_exp))
    g2 = sum(jnp.where(i2 == e, srow[e], 0.0) for e in range(n_exp))
    tot = g1 + g2
    eid_ref[0:1, :] = i1
    eid_ref[1:2, :] = i2
    gate_ref[0:1, :] = g1 / tot
    gate_ref[1:2, :] = g2 / tot

    @pl.when(pl.program_id(0) == 0)
    def _():
        cnt_ref[...] = jnp.zeros_like(cnt_ref)

    tm = x_ref.shape[0]
    eidx = lax.broadcasted_iota(jnp.int32, (n_exp, tm), 0)
    cnt = cnt_ref[...]
    for k, ik in enumerate((i1, i2)):
        hot = eidx == ik
        hot_f = jnp.where(hot, 1.0, 0.0)
        earlier = _dot(hot_f.astype(BF16), before_ref[...])
        rank = jnp.sum(jnp.where(hot, cnt + earlier, 0.0), axis=0, keepdims=True)
        rank_ref[k:k + 1, :] = rank.astype(jnp.int32)
        cnt = cnt + jnp.sum(hot_f, axis=1, keepdims=True)
    cnt_ref[...] = cnt
    count_ref[...] = jnp.broadcast_to(cnt, count_ref.shape).astype(jnp.int32)


def router(x, w_router, router_bias, tm):
    T, D = x.shape
    E = w_router.shape[1]
    s = lax.broadcasted_iota(jnp.int32, (tm, tm), 0)
    t = lax.broadcasted_iota(jnp.int32, (tm, tm), 1)
    before = jnp.where(s < t, 1.0, 0.0).astype(BF16)
    slots = pl.BlockSpec((2, tm), lambda i: (0, i))
    eid, gate, rank, counts = pl.pallas_call(
        _router_kernel,
        grid=(T // tm,),
        in_specs=[pl.BlockSpec((tm, D), lambda i: (i, 0)),
                  pl.BlockSpec((E, D), lambda i: (0, 0)),
                  pl.BlockSpec((E, 1), lambda i: (0, 0)),
                  pl.BlockSpec((tm, tm), lambda i: (0, 0))],
        out_specs=[slots, slots, slots, pl.BlockSpec((E, V7X_LANES), lambda i: (0, 0))],
        out_shape=[jax.ShapeDtypeStruct((2, T), jnp.int32), jax.ShapeDtypeStruct((2, T), F32),
                   jax.ShapeDtypeStruct((2, T), jnp.int32),
                   jax.ShapeDtypeStruct((E, V7X_LANES), jnp.int32)],
        scratch_shapes=[pltpu.VMEM((E, 1), F32)],
        compiler_params=_params("arbitrary"),
        name="router",
    )(x, w_router.T, router_bias.reshape(E, 1).astype(F32), before)
    return eid, gate, rank, counts[:, 0]


def _gelu_tanh(x):
    c = math.sqrt(2.0 / math.pi)
    return 0.5 * x * (1.0 + jnp.tanh(c * (x + 0.044715 * (x * x * x))))


ROW_COPY_UNROLL = 8


def _row_copies(pos_ref, step, tm, make):
    n_tok = pos_ref.shape[0] // 2

    def copy(r, k):
        return make(r, k, pos_ref[k * n_tok + step * tm + r])

    def start(r, c):
        copy(r, 0).start()
        copy(r, 1).start()
        return c

    def wait(r, c):
        copy(r, 0).wait()
        copy(r, 1).wait()
        return c

    lax.fori_loop(0, tm, start, 0, unroll=ROW_COPY_UNROLL)
    lax.fori_loop(0, tm, wait, 0, unroll=ROW_COPY_UNROLL)


def _dispatch_kernel(pos_ref, x_ref, zeros_ref, xs_ref, sem):
    del zeros_ref
    tm = x_ref.shape[0]

    def make(r, k, p):
        return pltpu.make_async_copy(x_ref.at[pl.ds(r, 1), :], xs_ref.at[pl.ds(p, 1), :], sem)

    _row_copies(pos_ref, pl.program_id(0), tm, make)


def _experts_kernel(te_ref, used_ref, xs_ref, wu_ref, wd_ref, ys_ref):
    del te_ref
    j = pl.program_id(0)

    @pl.when(j < used_ref[0])
    def _():
        h = _gelu_tanh(_dot(xs_ref[...].astype(BF16), wu_ref[...]))
        ys_ref[...] = _dot(h.astype(BF16), wd_ref[...])

    @pl.when(j >= used_ref[0])
    def _():
        ys_ref[...] = jnp.zeros_like(ys_ref)


def _combine_kernel(pos_ref, x_ref, gate_ref, ys_ref, g_ref, b_ref, o_ref, buf_ref, sem, *, alpha):
    tm = x_ref.shape[0]

    def make(r, k, p):
        return pltpu.make_async_copy(ys_ref.at[pl.ds(p, 1), :], buf_ref.at[k, pl.ds(r, 1), :], sem)

    _row_copies(pos_ref, pl.program_id(0), tm, make)
    gate = gate_ref[...]
    y = gate[:, 0:1] * buf_ref[0] + gate[:, 1:2] * buf_ref[1]
    o_ref[...] = _post_ln(x_ref[...], y, g_ref[...], b_ref[...], alpha)


def _moe_plan(eid, rank, counts, tile, n_tiles):
    padded = (counts + tile - 1) // tile * tile
    ends = jnp.cumsum(padded)
    pos = (ends - padded)[eid] + rank
    tile_ends = ends // tile
    used = tile_ends[-1]
    j = jnp.arange(n_tiles, dtype=jnp.int32)
    tile_expert = jnp.sum(tile_ends[None, :] <= jnp.minimum(j, used - 1)[:, None], axis=1)
    return pos.astype(jnp.int32), tile_expert.astype(jnp.int32), used.astype(jnp.int32).reshape(1)


def _dma_params():
    return pltpu.CompilerParams(dimension_semantics=("arbitrary",), vmem_limit_bytes=VMEM_LIMIT,
                                disable_bounds_checks=True)


def moe_sparse(x, eid, gate, rank, counts, w_up_bf, w_down_bf, g, b, alpha, tm, tile):
    T, D = x.shape
    E, _, F = w_up_bf.shape
    n_tiles = (2 * T + tile - 1) // tile + E
    pos, tile_expert, used = _moe_plan(eid, rank, counts, tile, n_tiles)
    pos_flat = pos.reshape(-1)
    any_spec = pl.BlockSpec(memory_space=pl.ANY)

    xs = pl.pallas_call(
        _dispatch_kernel,
        grid_spec=pltpu.PrefetchScalarGridSpec(
            num_scalar_prefetch=1,
            grid=(T // tm,),
            in_specs=[pl.BlockSpec((tm, D), lambda i, pos: (i, 0)), any_spec],
            out_specs=any_spec,
            scratch_shapes=[pltpu.SemaphoreType.DMA(())],
        ),
        out_shape=jax.ShapeDtypeStruct((n_tiles * tile, D), F32),
        input_output_aliases={2: 0},
        compiler_params=_dma_params(),
        name="moe_dispatch",
    )(pos_flat, x, jnp.zeros((n_tiles * tile, D), F32))

    ys = pl.pallas_call(
        _experts_kernel,
        grid_spec=pltpu.PrefetchScalarGridSpec(
            num_scalar_prefetch=2,
            grid=(n_tiles,),
            in_specs=[pl.BlockSpec((tile, D), lambda j, te, used: (jnp.minimum(j, used[0] - 1), 0)),
                      pl.BlockSpec((None, D, F), lambda j, te, used: (te[j], 0, 0)),
                      pl.BlockSpec((None, F, D), lambda j, te, used: (te[j], 0, 0))],
            out_specs=pl.BlockSpec((tile, D), lambda j, te, used: (j, 0)),
        ),
        out_shape=jax.ShapeDtypeStruct((n_tiles * tile, D), F32),
        compiler_params=_params("arbitrary"),
        name="moe_experts",
    )(tile_expert, used, xs, w_up_bf, w_down_bf)

    vec = pl.BlockSpec((1, D), lambda i, pos: (0, 0))
    row = pl.BlockSpec((tm, D), lambda i, pos: (i, 0))
    return pl.pallas_call(
        functools.partial(_combine_kernel, alpha=alpha),
        grid_spec=pltpu.PrefetchScalarGridSpec(
            num_scalar_prefetch=1,
            grid=(T // tm,),
            in_specs=[row, pl.BlockSpec((tm, 2), lambda i, pos: (i, 0)), any_spec, vec, vec],
            out_specs=row,
            scratch_shapes=[pltpu.VMEM((2, tm, D), F32), pltpu.SemaphoreType.DMA(())],
        ),
        out_shape=jax.ShapeDtypeStruct((T, D), F32),
        compiler_params=_dma_params(),
        name="moe_combine",
    )(pos_flat, x, gate.T, ys, g.reshape(1, D), b.reshape(1, D))


def _matmul_kernel(x_ref, w_ref, o_ref):
    o_ref[...] = _dot(x_ref[...].astype(BF16), w_ref[...])


def matmul(x, w_bf, tm):
    T, K = x.shape
    N = w_bf.shape[1]
    return pl.pallas_call(
        _matmul_kernel,
        grid=(T // tm,),
        in_specs=[pl.BlockSpec((tm, K), lambda i: (i, 0)), pl.BlockSpec((K, N), lambda i: (0, 0))],
        out_specs=pl.BlockSpec((tm, N), lambda i: (i, 0)),
        out_shape=jax.ShapeDtypeStruct((T, N), F32),
        compiler_params=_params("parallel"),
        name="matmul",
    )(x, w_bf)


def _pool_windows(ext_ref, tm, pos0, o_ref):
    C = ext_ref.shape[1]
    gw = C // len(POOL_WINDOWS)
    pos = pos0 + lax.broadcasted_iota(jnp.int32, (tm, 1), 0)
    for g, w in enumerate(POOL_WINDOWS):
        cols = slice(g * gw, (g + 1) * gw)
        cur = ext_ref[pl.ds(POOL_PAD, tm), cols]
        tot = cur
        for d in range(1, w):
            tot = tot + ext_ref[pl.ds(POOL_PAD - d, tm), cols]
        cnt = jnp.minimum(pos + 1, w).astype(F32)
        o_ref[:, cols] = (tot / cnt - cur).astype(o_ref.dtype)


def _pool_prompt_kernel(u_ref, o_ref, ext_ref):
    i = pl.program_id(0)
    tm = u_ref.shape[0]

    @pl.when(i == 0)
    def _():
        ext_ref[pl.ds(0, POOL_PAD), :] = jnp.zeros((POOL_PAD, ext_ref.shape[1]), F32)

    ext_ref[pl.ds(POOL_PAD, tm), :] = u_ref[...]
    _pool_windows(ext_ref, tm, i * tm, o_ref)
    ext_ref[pl.ds(0, POOL_PAD), :] = ext_ref[pl.ds(tm, POOL_PAD), :]


def pool_prompt(u, tm):
    T, C = u.shape
    spec = pl.BlockSpec((tm, C), lambda i: (i, 0))
    return pl.pallas_call(
        _pool_prompt_kernel,
        grid=(T // tm,),
        in_specs=[spec],
        out_specs=spec,
        out_shape=jax.ShapeDtypeStruct((T, C), BF16),
        scratch_shapes=[pltpu.VMEM((POOL_PAD + tm, C), F32)],
        compiler_params=_params("arbitrary"),
        name="pool_prompt",
    )(u)


def _pool_sample_kernel(ext_ref, o_ref, *, n_new, start_pos):
    _pool_windows(ext_ref, n_new, start_pos, o_ref)


def pool_sample(ext, n_new, start_pos):
    B, R, C = ext.shape
    assert R == POOL_PAD + n_new
    return pl.pallas_call(
        functools.partial(_pool_sample_kernel, n_new=n_new, start_pos=start_pos),
        grid=(B,),
        in_specs=[pl.BlockSpec((None, R, C), lambda b: (b, 0, 0))],
        out_specs=pl.BlockSpec((None, n_new, C), lambda b: (b, 0, 0)),
        out_shape=jax.ShapeDtypeStruct((B, n_new, C), F32),
        compiler_params=_params("parallel"),
        name="pool_sample",
    )(ext)


def _out_odd_kernel(x_ref, d_ref, wg_ref, sc_ref, wo_ref, g_ref, b_ref, o_ref, *, alpha):
    d = d_ref[...].astype(BF16)
    ng, gw, _ = wg_ref.shape
    ys = [_dot(d[:, g * gw:(g + 1) * gw], wg_ref[g]) for g in range(ng)]
    y = (jnp.concatenate(ys, axis=1) * sc_ref[...]).astype(BF16)
    o_ref[...] = _post_ln(x_ref[...], _dot(y, wo_ref[...]), g_ref[...], b_ref[...], alpha)


def out_odd(x, d, w_grp_bf, scale, w_out_bf, g, b, alpha, tm):
    T, D = x.shape
    row = pl.BlockSpec((tm, D), lambda i: (i, 0))
    full = lambda a: pl.BlockSpec(a.shape, lambda i: (0,) * a.ndim)
    sc, g2, b2 = scale.reshape(1, D), g.reshape(1, D), b.reshape(1, D)
    return pl.pallas_call(
        functools.partial(_out_odd_kernel, alpha=alpha),
        grid=(T // tm,),
        in_specs=[row, row, full(w_grp_bf), full(sc), full(w_out_bf), full(g2), full(b2)],
        out_specs=row,
        out_shape=jax.ShapeDtypeStruct((T, D), F32),
        compiler_params=_params("parallel"),
        name="out_odd",
    )(x, d, w_grp_bf, sc, w_out_bf, g2, b2)


TM_PROMPT = 512
TQ_PROMPT = 512
TK_PROMPT = 256
DECODE_PAGES = 8
MOE_TILE_PROMPT = 512
MOE_TILE_SAMPLE = 128


def _ffn(x, w_router, router_bias, w_up_bf, w_down_bf, g, b, alpha, tm, tile):
    eid, gate, rank, counts = router(x, w_router, router_bias, tm)
    return moe_sparse(x, eid, gate, rank, counts, w_up_bf, w_down_bf, g, b, alpha, tm, tile)


def kernel(x_prompt, x_sample, cache_k, cache_v, state_conv, state_pool, page_table, w_in_ab, conv_w, sb_bias, w_out_ab, w_in_c, w_grp_c, scale_c, w_out_c, w_router, router_bias, w_exp_up, w_exp_down, ln_mix_g, ln_mix_b, ln_ffn_g, ln_ffn_b):
    bp, sp_len, D = x_prompt.shape
    bs, ls, _ = x_sample.shape
    assert bp == 1
    depth = w_exp_up.shape[0]
    n_heads = sb_bias.shape[1]
    cw = conv_w.shape[2]
    aw = n_heads * HEAD_DIM
    n_pages = page_table.shape[1]
    past_len = n_pages * PAGE
    alpha = (2 * depth) ** 0.25
    Tp, Ts = bp * sp_len, bs * ls

    xp = x_prompt.reshape(Tp, D)
    xs = x_sample.reshape(Ts, D)
    outs = {n: [] for n in ("kp", "vp", "ks", "vs", "cp", "cs", "pp", "ps")}

    for layer in range(depth):
        w_up_bf = w_exp_up[layer].astype(BF16)
        w_down_bf = w_exp_down[layer].astype(BF16)
        if layer % 2 == 0:
            e = layer // 2
            w_in_bf = w_in_ab[e].astype(BF16)
            w_out_bf = w_out_ab[e].astype(BF16)
            yc, h, q, k, v, kb, vb = proj_even(xp, w_in_bf, conv_w[e], None, sp_len, TM_PROMPT, True)
            att = sb_prompt(q, kb, vb, sb_bias[e], TQ_PROMPT, TK_PROMPT)
            xp = out_even(xp, yc, att, w_out_bf, ln_mix_g[layer], ln_mix_b[layer], alpha, TM_PROMPT)
            outs["kp"].append(k.reshape(bp, sp_len, n_heads, HEAD_DIM))
            outs["vp"].append(v.reshape(bp, sp_len, n_heads, HEAD_DIM))
            outs["cp"].append(h.reshape(bp, sp_len, cw)[:, sp_len - 2:])
            st = state_conv[e]
            zero = jnp.zeros((bs, 1, cw), F32)
            p1 = jnp.concatenate([st[:, 1:2]] + [zero] * (ls - 1), axis=1).reshape(Ts, cw)
            p2 = jnp.concatenate([st[:, 0:1], st[:, 1:2]] + [zero] * (ls - 2), axis=1).reshape(Ts, cw)
            yc, h, q, k, v, kb, vb = proj_even(xs, w_in_bf, conv_w[e], (p1, p2), ls, Ts, False)
            att = sb_decode(q.reshape(bs, ls, aw), kb.reshape(bs, ls, aw), vb.reshape(bs, ls, aw),
                            cache_k.reshape(-1, PAGE, aw), cache_v.reshape(-1, PAGE, aw),
                            page_table + e * cache_k.shape[1], sb_bias[e], n_heads,
                            DECODE_PAGES).reshape(Ts, aw)
            xs = out_even(xs, yc, att, w_out_bf, ln_mix_g[layer], ln_mix_b[layer], alpha, Ts)
            outs["ks"].append(k.reshape(bs, ls, n_heads, HEAD_DIM))
            outs["vs"].append(v.reshape(bs, ls, n_heads, HEAD_DIM))
            hs = jnp.concatenate([st, h.reshape(bs, ls, cw)], axis=1)
            outs["cs"].append(hs[:, -2:])
        else:
            o = layer // 2
            w_in_bf = w_in_c[o].astype(BF16)
            w_grp_bf = w_grp_c[o].astype(BF16)
            w_out_bf = w_out_c[o].astype(BF16)
            buf = POOL_PAD - 1
            u = matmul(xp, w_in_bf, TM_PROMPT)
            d = pool_prompt(u, TM_PROMPT)
            xp = out_odd(xp, d, w_grp_bf, scale_c[o], w_out_bf,
                                ln_mix_g[layer], ln_mix_b[layer], alpha, TM_PROMPT)
            up = jnp.concatenate([jnp.zeros((bp, buf, D), F32), u.reshape(bp, sp_len, D)], axis=1)
            outs["pp"].append(up[:, -buf:])
            u = matmul(xs, w_in_bf, Ts)
            ext = jnp.concatenate([jnp.zeros((bs, 1, D), F32), state_pool[o], u.reshape(bs, ls, D)], axis=1)
            d = pool_sample(ext, ls, past_len).reshape(Ts, D)
            xs = out_odd(xs, d, w_grp_bf, scale_c[o], w_out_bf,
                                ln_mix_g[layer], ln_mix_b[layer], alpha, Ts)
            outs["ps"].append(ext[:, -buf:])
        xp = _ffn(xp, w_router, router_bias, w_up_bf, w_down_bf,
                  ln_ffn_g[layer], ln_ffn_b[layer], alpha, TM_PROMPT, MOE_TILE_PROMPT)
        xs = _ffn(xs, w_router, router_bias, w_up_bf, w_down_bf,
                  ln_ffn_g[layer], ln_ffn_b[layer], alpha, Ts, MOE_TILE_SAMPLE)

    st = lambda n: jnp.stack(outs[n])
    return (xp.reshape(bp, sp_len, D), xs.reshape(bs, ls, D),
            st("kp"), st("vp"), st("ks"), st("vs"), st("cp"), st("cs"), st("pp"), st("ps"))
```

```python
import functools
import math

import jax
import jax.numpy as jnp
from jax import lax
from jax.experimental import pallas as pl
from jax.experimental.pallas import tpu as pltpu

V7X_LANES = 128
V7X_SUBLANES = 8
V7X_VMEM_BYTES = 64 * 1024 * 1024
VMEM_LIMIT = V7X_VMEM_BYTES - 8 * 1024 * 1024

PAGE = 128
HEAD_DIM = 64
HEADS_PER_BLOCK = V7X_LANES // HEAD_DIM
POOL_WINDOWS = (2, 4, 8, 16)
POOL_PAD = 16
LN_EPS = 1e-5
N_GROUPS = 4
GROUP_SIZE = 4

BF16 = jnp.bfloat16
F32 = jnp.float32


def _params(*sem):
    return pltpu.CompilerParams(dimension_semantics=sem, vmem_limit_bytes=VMEM_LIMIT)


def _dot(a, b):
    return jnp.dot(a, b, preferred_element_type=F32)


def _dot_nt(a, b):
    return lax.dot_general(a, b, (((1,), (1,)), ((), ())), preferred_element_type=F32)


LOG2E = math.log2(math.e)


SOFTPLUS2_CLAMP = 100.0


def _softplus2(z2):
    return jnp.maximum(z2, jnp.log2(1.0 + jnp.exp2(jnp.minimum(z2, SOFTPLUS2_CLAMP))))


def _post_ln(x, sub, g, b, alpha):
    y = alpha * x + sub
    mu = jnp.mean(y, axis=-1, keepdims=True)
    yc = y - mu
    var = jnp.mean(yc * yc, axis=-1, keepdims=True)
    return yc * lax.rsqrt(var + LN_EPS) * g + b


def _store_head_expanded(o_ref, x):
    lane = lax.broadcasted_iota(jnp.int32, (1, V7X_LANES), 1)
    for h in range(x.shape[1] // HEAD_DIM):
        p, hh = divmod(h, HEADS_PER_BLOCK)
        blk = x[:, p * V7X_LANES:(p + 1) * V7X_LANES]
        o_ref[:, h * V7X_LANES:(h + 1) * V7X_LANES] = jnp.where(
            (lane // HEAD_DIM) == hh, blk, 0.0).astype(o_ref.dtype)


def _proj_even_kernel(*refs, cw, seq_len, has_state, expand_heads):
    if has_state:
        (x_ref, w_ref, cwt_ref, p1_ref, p2_ref,
         yc_ref, h_ref, q_ref, k_ref, v_ref, kb_ref, vb_ref, carry_ref) = refs
    else:
        (x_ref, w_ref, cwt_ref,
         yc_ref, h_ref, q_ref, k_ref, v_ref, kb_ref, vb_ref, carry_ref) = refs
    i = pl.program_id(0)
    tm = x_ref.shape[0]
    xb = x_ref[...].astype(BF16)

    def sec(n):
        return _dot(xb, w_ref[:, n * cw:(n + 1) * cw])

    gb, gc, xv = sec(0), sec(1), sec(2)
    h = gc * xv
    hm1 = pltpu.roll(h, 1, axis=0)
    hm2 = pltpu.roll(h, 2, axis=0)
    row = lax.broadcasted_iota(jnp.int32, (tm, 1), 0)
    if has_state:
        l = row % seq_len
        hm1 = jnp.where(l < 1, p1_ref[...], hm1)
        hm2 = jnp.where(l < 2, p2_ref[...], hm2)
    else:
        @pl.when(i == 0)
        def _():
            carry_ref[...] = jnp.zeros_like(carry_ref)
        prev = carry_ref[...]
        last = V7X_SUBLANES - 1
        hm1 = jnp.where(row == 0, prev[last:last + 1, :], hm1)
        hm2 = jnp.where(row == 0, prev[last - 1:last, :], hm2)
        hm2 = jnp.where(row == 1, prev[last:last + 1, :], hm2)
        carry_ref[...] = h[tm - V7X_SUBLANES:, :]
    cwt = cwt_ref[...]
    conv = cwt[0:1, :] * hm2 + cwt[1:2, :] * hm1 + cwt[2:3, :] * h
    yc_ref[...] = (gb * conv).astype(BF16)
    h_ref[...] = h
    q_ref[...] = (sec(3) * (HEAD_DIM ** -0.5 * LOG2E)).astype(BF16)
    k = sec(4)
    v = sec(5)
    k_ref[...] = k
    v_ref[...] = v
    if expand_heads:
        _store_head_expanded(kb_ref, k)
        _store_head_expanded(vb_ref, v)
    else:
        kb_ref[...] = k.astype(BF16)
        vb_ref[...] = v.astype(BF16)


def proj_even(x, w_bf, conv_w, state_rows, seq_len, tm, expand_heads):
    T, D = x.shape
    cw = conv_w.shape[1]
    assert w_bf.shape == (D, 6 * cw) and T % tm == 0
    has_state = state_rows is not None
    if has_state:
        assert T == tm
    bw = cw * HEADS_PER_BLOCK if expand_heads else cw
    row_in = pl.BlockSpec((tm, D), lambda i: (i, 0))
    row_out = pl.BlockSpec((tm, cw), lambda i: (i, 0))
    row_kv = pl.BlockSpec((tm, bw), lambda i: (i, 0))
    in_specs = [row_in,
                pl.BlockSpec((D, 6 * cw), lambda i: (0, 0)),
                pl.BlockSpec((3, cw), lambda i: (0, 0))]
    args = [x, w_bf, conv_w]
    if has_state:
        in_specs += [row_out, row_out]
        args += list(state_rows)
    out_shape = [jax.ShapeDtypeStruct((T, cw), d) for d in (BF16, F32, BF16, F32, F32)]
    out_shape += [jax.ShapeDtypeStruct((T, bw), BF16)] * 2
    return pl.pallas_call(
        functools.partial(_proj_even_kernel, cw=cw, seq_len=seq_len, has_state=has_state,
                          expand_heads=expand_heads),
        grid=(T // tm,),
        in_specs=in_specs,
        out_specs=[row_out] * 5 + [row_kv] * 2,
        out_shape=out_shape,
        scratch_shapes=[pltpu.VMEM((V7X_SUBLANES, cw), F32)],
        compiler_params=_params("arbitrary"),
        name="proj_even",
    )(*args)


def _sb_weights(z, nu, rs, mask):
    tk = nu.shape[0]
    groups = z.shape[1] // tk
    cols = lambda x, g: x[:, g * tk:(g + 1) * tk]
    sp = _softplus2(z)
    if mask is not None:
        mask = jnp.concatenate([mask] * groups, axis=1)
        spm = jnp.where(mask, sp, 0.0)
    else:
        spm = sp
    hi = spm.astype(BF16)
    off = [_dot(cols(hi, g), nu) + rs[g] for g in range(groups)]
    a = jnp.exp2((z - sp) + jnp.concatenate(off, axis=1))
    if mask is not None:
        a = jnp.where(mask, a, 0.0)
    return a.astype(BF16), [rs[g] - jnp.sum(cols(spm, g), axis=1, keepdims=True) for g in range(groups)]


def _sb_prompt_kernel(bias_ref, q_ref, k_ref, v_ref, nu_ref, o_ref, acc_ref, r_ref, *, tk):
    p = pl.program_id(0)
    i = pl.program_id(1)
    tq = q_ref.shape[0]
    nd = tq // tk
    groups = HEADS_PER_BLOCK
    col = lax.broadcasted_iota(jnp.int32, (1, groups * tk), 1)
    bias = jnp.zeros((1, groups * tk), F32)
    for g in range(groups):
        bias = jnp.where(col // tk == g, bias_ref[groups * p + g] * LOG2E, bias)
    acc_ref[...] = jnp.zeros_like(acc_ref)
    r_ref[...] = jnp.zeros_like(r_ref)

    def rows(ref, start):
        return jnp.concatenate([ref[pl.ds(start, tk), g * V7X_LANES:(g + 1) * V7X_LANES]
                                for g in range(groups)], axis=0)

    def step(start, mask):
        z = _dot_nt(q_ref[...], rows(k_ref, start)) + bias
        a, rs = _sb_weights(z, nu_ref[...], [r_ref[g] for g in range(groups)], mask)
        acc_ref[...] += _dot(a, rows(v_ref, start))
        for g in range(groups):
            r_ref[g] = rs[g]

    qpos = i * tq + lax.broadcasted_iota(jnp.int32, (tq, 1), 0)
    for d in reversed(range(nd)):
        start = pl.multiple_of((i * nd + d) * tk, tk)
        kpos = start + lax.broadcasted_iota(jnp.int32, (1, tk), 1)
        step(start, kpos < qpos)

    def body(jj, carry):
        step(pl.multiple_of((i * nd - 1 - jj) * tk, tk), None)
        return carry

    lax.fori_loop(0, i * nd, body, 0)
    o_ref[...] = acc_ref[...].astype(o_ref.dtype)


def _neg_upper(n):
    j = lax.broadcasted_iota(jnp.int32, (n, n), 0)
    s = lax.broadcasted_iota(jnp.int32, (n, n), 1)
    return jnp.where(j > s, -1.0, 0.0).astype(BF16)


def sb_prompt(q_bf, kx_bf, vx_bf, bias, tq, tk):
    T, W = q_bf.shape
    assert T % tq == 0 and tq % tk == 0 and W % V7X_LANES == 0
    assert kx_bf.shape == (T, HEADS_PER_BLOCK * W)
    nblk = W // V7X_LANES
    qspec = pl.BlockSpec((tq, V7X_LANES), lambda p, i: (i, p))
    kvspec = pl.BlockSpec((T, HEADS_PER_BLOCK * V7X_LANES), lambda p, i: (0, p))
    return pl.pallas_call(
        functools.partial(_sb_prompt_kernel, tk=tk),
        grid=(nblk, T // tq),
        in_specs=[pl.BlockSpec(memory_space=pltpu.SMEM), qspec, kvspec, kvspec,
                  pl.BlockSpec((tk, tk), lambda p, i: (0, 0))],
        out_specs=qspec,
        out_shape=jax.ShapeDtypeStruct((T, W), BF16),
        scratch_shapes=[pltpu.VMEM((tq, V7X_LANES), F32),
                        pltpu.VMEM((HEADS_PER_BLOCK, tq, 1), F32)],
        compiler_params=_params("arbitrary", "arbitrary"),
        name="sb_prompt",
    )(bias, q_bf, kx_bf, vx_bf, _neg_upper(tk))


def _sb_decode_kernel(pt_ref, qbd_ref, brow_ref, knt_ref, vnt_ref, nu_ref, *rest, pages, n_new, n_heads):
    kt_refs = rest[:pages]
    vt_refs = rest[pages:2 * pages]
    o_ref, acc_ref, r_ref = rest[2 * pages:]
    j = pl.program_id(1)
    qbd = qbd_ref[...]
    brow = brow_ref[...]
    nu = nu_ref[...]
    tk = nu.shape[0]
    rows = qbd.shape[0]

    @pl.when(j == 0)
    def _():
        l = lax.broadcasted_iota(jnp.int32, (rows, 1), 0) // n_heads
        c = lax.broadcasted_iota(jnp.int32, (1, PAGE), 1)
        z = _dot(qbd, knt_ref[...]) + brow
        a, (r,) = _sb_weights(z, nu[:PAGE, :PAGE], [jnp.zeros((rows, 1), F32)], c < l)
        acc_ref[...] = _dot_nt(a, vnt_ref[...])
        r_ref[...] = r

    kt = jnp.concatenate([ref[...].astype(BF16) for ref in kt_refs], axis=1)
    z = _dot(qbd, kt) + brow
    r = r_ref[...]
    parts = []
    for c in reversed(range(pages * PAGE // tk)):
        a, (r,) = _sb_weights(z[:, c * tk:(c + 1) * tk], nu, [r], None)
        parts.append(a)
    vt = jnp.concatenate([ref[...].astype(BF16) for ref in vt_refs], axis=1)
    acc_ref[...] += _dot_nt(jnp.concatenate(parts[::-1], axis=1), vt)
    r_ref[...] = r

    @pl.when(j == pl.num_programs(1) - 1)
    def _():
        acc = acc_ref[...]
        rr = lax.broadcasted_iota(jnp.int32, acc.shape, 0) % n_heads
        cc = lax.broadcasted_iota(jnp.int32, acc.shape, 1) // HEAD_DIM
        accm = jnp.where(rr == cc, acc, 0.0)
        o_ref[...] = jnp.sum(accm.reshape(n_new, n_heads, acc.shape[1]), axis=1).astype(o_ref.dtype)


def sb_decode(q_bf, k_new_bf, v_new_bf, cache_kt, cache_vt, page_table, bias, n_heads, pages, tk):
    B, L, W = q_bf.shape
    n_pages = page_table.shape[1]
    assert n_pages % pages == 0 and W == n_heads * HEAD_DIM and L <= PAGE
    assert (pages * PAGE) % tk == 0 and tk % PAGE == 0
    rows = L * n_heads
    col_head = jnp.arange(W, dtype=jnp.int32) // HEAD_DIM
    keep = (jnp.arange(rows, dtype=jnp.int32) % n_heads)[:, None] == col_head[None, :]
    qbd = jnp.where(keep[None], jnp.repeat(q_bf, n_heads, axis=1), jnp.zeros((), BF16))
    brow = jnp.tile(bias.astype(F32) * LOG2E, L).reshape(rows, 1)
    pad = ((0, 0), (0, PAGE - L), (0, 0))
    knt = jnp.transpose(jnp.pad(k_new_bf, pad), (0, 2, 1))
    vnt = jnp.transpose(jnp.pad(v_new_bf, pad), (0, 2, 1))
    steps = n_pages // pages

    def page_map(t):
        def index(b, j, pt):
            return (pt[b * n_pages + (steps - 1 - j) * pages + t], 0, 0)
        return index

    per_b = lambda shape: pl.BlockSpec((None,) + shape, lambda b, j, pt: (b, 0, 0))
    page_specs = [pl.BlockSpec((None, W, PAGE), page_map(t)) for t in range(pages)]
    grid_spec = pltpu.PrefetchScalarGridSpec(
        num_scalar_prefetch=1,
        grid=(B, steps),
        in_specs=[per_b((rows, W)),
                  pl.BlockSpec((rows, 1), lambda b, j, pt: (0, 0)),
                  per_b((W, PAGE)), per_b((W, PAGE)),
                  pl.BlockSpec((tk, tk), lambda b, j, pt: (0, 0))] + page_specs + page_specs,
        out_specs=per_b((L, W)),
        scratch_shapes=[pltpu.VMEM((rows, W), F32), pltpu.VMEM((rows, 1), F32)],
    )
    return pl.pallas_call(
        functools.partial(_sb_decode_kernel, pages=pages, n_new=L, n_heads=n_heads),
        grid_spec=grid_spec,
        out_shape=jax.ShapeDtypeStruct((B, L, W), F32),
        compiler_params=_params("arbitrary", "arbitrary"),
        name="sb_decode",
    )(page_table.reshape(-1), qbd, brow, knt, vnt, _neg_upper(tk),
      *([cache_kt] * pages), *([cache_vt] * pages))


def _out_even_kernel(x_ref, yc_ref, att_ref, wt_ref, wb_ref, g_ref, b_ref, o_ref, *, alpha):
    m = _dot(yc_ref[...], wt_ref[...]) + _dot(att_ref[...].astype(BF16), wb_ref[...])
    o_ref[...] = _post_ln(x_ref[...], m, g_ref[...], b_ref[...], alpha)


def out_even(x, yc, att, w_out_bf, g, b, alpha, tm):
    T, D = x.shape
    cw = yc.shape[1]
    row = lambda w: pl.BlockSpec((tm, w), lambda i: (i, 0))
    full = lambda a: pl.BlockSpec(a.shape, lambda i: (0,) * a.ndim)
    wt, wb = w_out_bf[:cw], w_out_bf[cw:]
    g2, b2 = g.reshape(1, D), b.reshape(1, D)
    return pl.pallas_call(
        functools.partial(_out_even_kernel, alpha=alpha),
        grid=(T // tm,),
        in_specs=[row(D), row(cw), row(att.shape[1]), full(wt), full(wb), full(g2), full(b2)],
        out_specs=row(D),
        out_shape=jax.ShapeDtypeStruct((T, D), F32),
        compiler_params=_params("parallel"),
        name="out_even",
    )(x, yc, att, wt, wb, g2, b2)


def _router_kernel(x_ref, wrt_ref, rb_ref, before_ref, eid_ref, gate_ref, rank_ref, count_ref, cnt_ref):
    n_exp = wrt_ref.shape[0]
    logits = lax.dot_general(wrt_ref[...], x_ref[...], (((1,), (1,)), ((), ())),
                             precision=lax.Precision.HIGHEST, preferred_element_type=F32)
    s = jax.nn.sigmoid(logits)
    sel = s + rb_ref[...]
    srow = [s[e:e + 1, :] for e in range(n_exp)]
    row = [sel[e:e + 1, :] for e in range(n_exp)]
    best_score, best_g = None, None
    for g in range(N_GROUPS):
        m = row[g * GROUP_SIZE:(g + 1) * GROUP_SIZE]
        score = None
        for a in range(GROUP_SIZE):
            for b in range(a + 1, GROUP_SIZE):
                pair = m[a] + m[b]
                score = pair if score is None else jnp.maximum(score, pair)
        if best_score is None:
            best_score, best_g = score, jnp.zeros_like(score, dtype=jnp.int32)
        else:
            upd = score > best_score
            best_score = jnp.where(upd, score, best_score)
            best_g = jnp.where(upd, g, best_g)
    neg = -jnp.inf
    masked = [jnp.where(best_g == (e // GROUP_SIZE), row[e], neg) for e in range(n_exp)]

    def first_argmax(vals):
        bv, bi = vals[0], jnp.zeros_like(best_g)
        for e in range(1, n_exp):
            upd = vals[e] > bv
            bv = jnp.where(upd, vals[e], bv)
            bi = jnp.where(upd, e, bi)
        return bi

    i1 = first_argmax(masked)
    i2 = first_argmax([jnp.where(i1 == e, neg, masked[e]) for e in range(n_exp)])
    g1 = sum(jnp.where(i1 == e, srow[e], 0.0) for e in range(n_exp))
    g2 = sum(jnp.where(i2 == e, srow[e], 0.0) for e in range(n_exp))
    tot = g1 + g2
    eid_ref[0:1, :] = i1
    eid_ref[1:2, :] = i2
    gate_ref[0:1, :] = g1 / tot
    gate_ref[1:2, :] = g2 / tot

    @pl.when(pl.program_id(0) == 0)
    def _():
        cnt_ref[...] = jnp.zeros_like(cnt_ref)

    tm = x_ref.shape[0]
    eidx = lax.broadcasted_iota(jnp.int32, (n_exp, tm), 0)
    cnt = cnt_ref[...]
    for k, ik in enumerate((i1, i2)):
        hot = eidx == ik
        hot_f = jnp.where(hot, 1.0, 0.0)
        earlier = _dot(hot_f.astype(BF16), before_ref[...])
        rank = jnp.sum(jnp.where(hot, cnt + earlier, 0.0), axis=0, keepdims=True)
        rank_ref[k:k + 1, :] = rank.astype(jnp.int32)
        cnt = cnt + jnp.sum(hot_f, axis=1, keepdims=True)
    cnt_ref[...] = cnt
    count_ref[...] = jnp.broadcast_to(cnt, count_ref.shape).astype(jnp.int32)


def router(x, w_router, router_bias, tm):
    T, D = x.shape
    E = w_router.shape[1]
    s = lax.broadcasted_iota(jnp.int32, (tm, tm), 0)
    t = lax.broadcasted_iota(jnp.int32, (tm, tm), 1)
    before = jnp.where(s < t, 1.0, 0.0).astype(BF16)
    slots = pl.BlockSpec((2, tm), lambda i: (0, i))
    eid, gate, rank, counts = pl.pallas_call(
        _router_kernel,
        grid=(T // tm,),
        in_specs=[pl.BlockSpec((tm, D), lambda i: (i, 0)),
                  pl.BlockSpec((E, D), lambda i: (0, 0)),
                  pl.BlockSpec((E, 1), lambda i: (0, 0)),
                  pl.BlockSpec((tm, tm), lambda i: (0, 0))],
        out_specs=[slots, slots, slots, pl.BlockSpec((E, V7X_LANES), lambda i: (0, 0))],
        out_shape=[jax.ShapeDtypeStruct((2, T), jnp.int32), jax.ShapeDtypeStruct((2, T), F32),
                   jax.ShapeDtypeStruct((2, T), jnp.int32),
                   jax.ShapeDtypeStruct((E, V7X_LANES), jnp.int32)],
        scratch_shapes=[pltpu.VMEM((E, 1), F32)],
        compiler_params=_params("arbitrary"),
        name="router",
    )(x, w_router.T, router_bias.reshape(E, 1).astype(F32), before)
    return eid, gate, rank, counts[:, 0]


def _gelu_tanh(x):
    c = math.sqrt(2.0 / math.pi)
    return 0.5 * x * (1.0 + jnp.tanh(c * (x + 0.044715 * (x * x * x))))


ROW_COPY_UNROLL = 8


def _row_copies(pos_ref, step, tm, make):
    n_tok = pos_ref.shape[0] // 2

    def copy(r, k):
        return make(r, k, pos_ref[k * n_tok + step * tm + r])

    def start(r, c):
        copy(r, 0).start()
        copy(r, 1).start()
        return c

    def wait(r, c):
        copy(r, 0).wait()
        copy(r, 1).wait()
        return c

    lax.fori_loop(0, tm, start, 0, unroll=ROW_COPY_UNROLL)
    lax.fori_loop(0, tm, wait, 0, unroll=ROW_COPY_UNROLL)


def _dispatch_kernel(pos_ref, x_ref, zeros_ref, xs_ref, sem):
    del zeros_ref
    tm = x_ref.shape[0]

    def make(r, k, p):
        return pltpu.make_async_copy(x_ref.at[pl.ds(r, 1), :], xs_ref.at[pl.ds(p, 1), :], sem)

    _row_copies(pos_ref, pl.program_id(0), tm, make)


def _experts_kernel(te_ref, used_ref, xs_ref, wu_ref, wd_ref, ys_ref):
    del te_ref
    j = pl.program_id(0)

    @pl.when(j < used_ref[0])
    def _():
        h = _gelu_tanh(_dot(xs_ref[...].astype(BF16), wu_ref[...]))
        ys_ref[...] = _dot(h.astype(BF16), wd_ref[...])

    @pl.when(j >= used_ref[0])
    def _():
        ys_ref[...] = jnp.zeros_like(ys_ref)


def _combine_kernel(pos_ref, x_ref, gate_ref, ys_ref, g_ref, b_ref, o_ref, buf_ref, sem, *, alpha):
    tm = x_ref.shape[0]

    def make(r, k, p):
        return pltpu.make_async_copy(ys_ref.at[pl.ds(p, 1), :], buf_ref.at[k, pl.ds(r, 1), :], sem)

    _row_copies(pos_ref, pl.program_id(0), tm, make)
    gate = gate_ref[...]
    y = gate[:, 0:1] * buf_ref[0] + gate[:, 1:2] * buf_ref[1]
    o_ref[...] = _post_ln(x_ref[...], y, g_ref[...], b_ref[...], alpha)


def _moe_plan(eid, rank, counts, tile, n_tiles):
    padded = (counts + tile - 1) // tile * tile
    ends = jnp.cumsum(padded)
    starts = ends - padded
    pos = rank + sum(jnp.where(eid == e, starts[e], 0) for e in range(counts.shape[0]))
    tile_ends = ends // tile
    used = tile_ends[-1]
    j = jnp.arange(n_tiles, dtype=jnp.int32)
    tile_expert = jnp.sum(tile_ends[None, :] <= jnp.minimum(j, used - 1)[:, None], axis=1)
    return pos.astype(jnp.int32), tile_expert.astype(jnp.int32), used.astype(jnp.int32).reshape(1)


def _dma_params():
    return pltpu.CompilerParams(dimension_semantics=("arbitrary",), vmem_limit_bytes=VMEM_LIMIT,
                                disable_bounds_checks=True)


def moe_sparse(x, eid, gate, rank, counts, w_up_bf, w_down_bf, layer, g, b, alpha, tm, tile):
    T, D = x.shape
    _, E, _, F = w_up_bf.shape
    n_tiles = (2 * T + tile - 1) // tile + E
    pos, tile_expert, used = _moe_plan(eid, rank, counts, tile, n_tiles)
    pos_flat = pos.reshape(-1)
    any_spec = pl.BlockSpec(memory_space=pl.ANY)

    xs = pl.pallas_call(
        _dispatch_kernel,
        grid_spec=pltpu.PrefetchScalarGridSpec(
            num_scalar_prefetch=1,
            grid=(T // tm,),
            in_specs=[pl.BlockSpec((tm, D), lambda i, pos: (i, 0)), any_spec],
            out_specs=any_spec,
            scratch_shapes=[pltpu.SemaphoreType.DMA(())],
        ),
        out_shape=jax.ShapeDtypeStruct((n_tiles * tile, D), F32),
        input_output_aliases={2: 0},
        compiler_params=_dma_params(),
        name="moe_dispatch",
    )(pos_flat, x, jnp.zeros((n_tiles * tile, D), F32))

    ys = pl.pallas_call(
        _experts_kernel,
        grid_spec=pltpu.PrefetchScalarGridSpec(
            num_scalar_prefetch=2,
            grid=(n_tiles,),
            in_specs=[pl.BlockSpec((tile, D), lambda j, te, used: (jnp.minimum(j, used[0] - 1), 0)),
                      pl.BlockSpec((None, None, D, F), lambda j, te, used: (layer, te[j], 0, 0)),
                      pl.BlockSpec((None, None, F, D), lambda j, te, used: (layer, te[j], 0, 0))],
            out_specs=pl.BlockSpec((tile, D), lambda j, te, used: (j, 0)),
        ),
        out_shape=jax.ShapeDtypeStruct((n_tiles * tile, D), F32),
        compiler_params=_params("arbitrary"),
        name="moe_experts",
    )(tile_expert, used, xs, w_up_bf, w_down_bf)

    vec = pl.BlockSpec((1, D), lambda i, pos: (0, 0))
    row = pl.BlockSpec((tm, D), lambda i, pos: (i, 0))
    return pl.pallas_call(
        functools.partial(_combine_kernel, alpha=alpha),
        grid_spec=pltpu.PrefetchScalarGridSpec(
            num_scalar_prefetch=1,
            grid=(T // tm,),
            in_specs=[row, pl.BlockSpec((tm, 2), lambda i, pos: (i, 0)), any_spec, vec, vec],
            out_specs=row,
            scratch_shapes=[pltpu.VMEM((2, tm, D), F32), pltpu.SemaphoreType.DMA(())],
        ),
        out_shape=jax.ShapeDtypeStruct((T, D), F32),
        compiler_params=_dma_params(),
        name="moe_combine",
    )(pos_flat, x, gate.T, ys, g.reshape(1, D), b.reshape(1, D))


def _matmul_kernel(x_ref, w_ref, o_ref):
    o_ref[...] = _dot(x_ref[...].astype(BF16), w_ref[...])


def matmul(x, w_bf, tm):
    T, K = x.shape
    N = w_bf.shape[1]
    return pl.pallas_call(
        _matmul_kernel,
        grid=(T // tm,),
        in_specs=[pl.BlockSpec((tm, K), lambda i: (i, 0)), pl.BlockSpec((K, N), lambda i: (0, 0))],
        out_specs=pl.BlockSpec((tm, N), lambda i: (i, 0)),
        out_shape=jax.ShapeDtypeStruct((T, N), F32),
        compiler_params=_params("parallel"),
        name="matmul",
    )(x, w_bf)


def _pool_windows(ext_ref, tm, pos0, o_ref):
    C = ext_ref.shape[1]
    gw = C // len(POOL_WINDOWS)
    pos = pos0 + lax.broadcasted_iota(jnp.int32, (tm, 1), 0)
    for g, w in enumerate(POOL_WINDOWS):
        cols = slice(g * gw, (g + 1) * gw)
        cur = ext_ref[pl.ds(POOL_PAD, tm), cols]
        tot = cur
        for d in range(1, w):
            tot = tot + ext_ref[pl.ds(POOL_PAD - d, tm), cols]
        cnt = jnp.minimum(pos + 1, w).astype(F32)
        o_ref[:, cols] = (tot / cnt - cur).astype(o_ref.dtype)


def _pool_prompt_kernel(u_ref, o_ref, ext_ref):
    i = pl.program_id(0)
    tm = u_ref.shape[0]

    @pl.when(i == 0)
    def _():
        ext_ref[pl.ds(0, POOL_PAD), :] = jnp.zeros((POOL_PAD, ext_ref.shape[1]), F32)

    ext_ref[pl.ds(POOL_PAD, tm), :] = u_ref[...]
    _pool_windows(ext_ref, tm, i * tm, o_ref)
    ext_ref[pl.ds(0, POOL_PAD), :] = ext_ref[pl.ds(tm, POOL_PAD), :]


def pool_prompt(u, tm):
    T, C = u.shape
    spec = pl.BlockSpec((tm, C), lambda i: (i, 0))
    return pl.pallas_call(
        _pool_prompt_kernel,
        grid=(T // tm,),
        in_specs=[spec],
        out_specs=spec,
        out_shape=jax.ShapeDtypeStruct((T, C), BF16),
        scratch_shapes=[pltpu.VMEM((POOL_PAD + tm, C), F32)],
        compiler_params=_params("arbitrary"),
        name="pool_prompt",
    )(u)


def _pool_sample_kernel(ext_ref, o_ref, *, n_new, start_pos):
    _pool_windows(ext_ref, n_new, start_pos, o_ref)


def pool_sample(ext, n_new, start_pos):
    B, R, C = ext.shape
    assert R == POOL_PAD + n_new
    return pl.pallas_call(
        functools.partial(_pool_sample_kernel, n_new=n_new, start_pos=start_pos),
        grid=(B,),
        in_specs=[pl.BlockSpec((None, R, C), lambda b: (b, 0, 0))],
        out_specs=pl.BlockSpec((None, n_new, C), lambda b: (b, 0, 0)),
        out_shape=jax.ShapeDtypeStruct((B, n_new, C), F32),
        compiler_params=_params("parallel"),
        name="pool_sample",
    )(ext)


def _out_odd_kernel(x_ref, d_ref, wg_ref, sc_ref, wo_ref, g_ref, b_ref, o_ref, *, alpha):
    d = d_ref[...].astype(BF16)
    ng, gw, _ = wg_ref.shape
    ys = [_dot(d[:, g * gw:(g + 1) * gw], wg_ref[g]) for g in range(ng)]
    y = (jnp.concatenate(ys, axis=1) * sc_ref[...]).astype(BF16)
    o_ref[...] = _post_ln(x_ref[...], _dot(y, wo_ref[...]), g_ref[...], b_ref[...], alpha)


def out_odd(x, d, w_grp_bf, scale, w_out_bf, g, b, alpha, tm):
    T, D = x.shape
    row = pl.BlockSpec((tm, D), lambda i: (i, 0))
    full = lambda a: pl.BlockSpec(a.shape, lambda i: (0,) * a.ndim)
    sc, g2, b2 = scale.reshape(1, D), g.reshape(1, D), b.reshape(1, D)
    return pl.pallas_call(
        functools.partial(_out_odd_kernel, alpha=alpha),
        grid=(T // tm,),
        in_specs=[row, row, full(w_grp_bf), full(sc), full(w_out_bf), full(g2), full(b2)],
        out_specs=row,
        out_shape=jax.ShapeDtypeStruct((T, D), F32),
        compiler_params=_params("parallel"),
        name="out_odd",
    )(x, d, w_grp_bf, sc, w_out_bf, g2, b2)


TM_PROMPT = 512
TQ_PROMPT = 512
TK_PROMPT = 256
DECODE_PAGES = 8
DECODE_TK = 256
MOE_TILE_PROMPT = 512
MOE_TILE_SAMPLE = 128


def _ffn(x, w_router, router_bias, w_up_bf, w_down_bf, layer, g, b, alpha, tm, tile):
    eid, gate, rank, counts = router(x, w_router, router_bias, tm)
    return moe_sparse(x, eid, gate, rank, counts, w_up_bf, w_down_bf, layer, g, b, alpha, tm, tile)


def kernel(x_prompt, x_sample, cache_k, cache_v, state_conv, state_pool, page_table, w_in_ab, conv_w, sb_bias, w_out_ab, w_in_c, w_grp_c, scale_c, w_out_c, w_router, router_bias, w_exp_up, w_exp_down, ln_mix_g, ln_mix_b, ln_ffn_g, ln_ffn_b):
    bp, sp_len, D = x_prompt.shape
    bs, ls, _ = x_sample.shape
    assert bp == 1
    depth = w_exp_up.shape[0]
    n_heads = sb_bias.shape[1]
    cw = conv_w.shape[2]
    aw = n_heads * HEAD_DIM
    n_pages = page_table.shape[1]
    past_len = n_pages * PAGE
    alpha = (2 * depth) ** 0.25
    Tp, Ts = bp * sp_len, bs * ls

    xp = x_prompt.reshape(Tp, D)
    xs = x_sample.reshape(Ts, D)
    outs = {n: [] for n in ("kp", "vp", "ks", "vs", "cp", "cs", "pp", "ps")}
    cache_kt = jnp.transpose(cache_k.reshape(-1, PAGE, aw), (0, 2, 1))
    cache_vt = jnp.transpose(cache_v.reshape(-1, PAGE, aw), (0, 2, 1))

    w_up_bf = w_exp_up.astype(BF16)
    w_down_bf = w_exp_down.astype(BF16)
    for layer in range(depth):
        if layer % 2 == 0:
            e = layer // 2
            w_in_bf = w_in_ab[e].astype(BF16)
            w_out_bf = w_out_ab[e].astype(BF16)
            yc, h, q, k, v, kb, vb = proj_even(xp, w_in_bf, conv_w[e], None, sp_len, TM_PROMPT, True)
            att = sb_prompt(q, kb, vb, sb_bias[e], TQ_PROMPT, TK_PROMPT)
            xp = out_even(xp, yc, att, w_out_bf, ln_mix_g[layer], ln_mix_b[layer], alpha, TM_PROMPT)
            outs["kp"].append(k.reshape(bp, sp_len, n_heads, HEAD_DIM))
            outs["vp"].append(v.reshape(bp, sp_len, n_heads, HEAD_DIM))
            outs["cp"].append(h.reshape(bp, sp_len, cw)[:, sp_len - 2:])
            st = state_conv[e]
            zero = jnp.zeros((bs, 1, cw), F32)
            p1 = jnp.concatenate([st[:, 1:2]] + [zero] * (ls - 1), axis=1).reshape(Ts, cw)
            p2 = jnp.concatenate([st[:, 0:1], st[:, 1:2]] + [zero] * (ls - 2), axis=1).reshape(Ts, cw)
            yc, h, q, k, v, kb, vb = proj_even(xs, w_in_bf, conv_w[e], (p1, p2), ls, Ts, False)
            att = sb_decode(q.reshape(bs, ls, aw), kb.reshape(bs, ls, aw), vb.reshape(bs, ls, aw),
                            cache_kt, cache_vt, page_table + e * cache_k.shape[1], sb_bias[e], n_heads,
                            DECODE_PAGES, DECODE_TK).reshape(Ts, aw)
            xs = out_even(xs, yc, att, w_out_bf, ln_mix_g[layer], ln_mix_b[layer], alpha, Ts)
            outs["ks"].append(k.reshape(bs, ls, n_heads, HEAD_DIM))
            outs["vs"].append(v.reshape(bs, ls, n_heads, HEAD_DIM))
            hs = jnp.concatenate([st, h.reshape(bs, ls, cw)], axis=1)
            outs["cs"].append(hs[:, -2:])
        else:
            o = layer // 2
            w_in_bf = w_in_c[o].astype(BF16)
            w_grp_bf = w_grp_c[o].astype(BF16)
            w_out_bf = w_out_c[o].astype(BF16)
            buf = POOL_PAD - 1
            u = matmul(xp, w_in_bf, TM_PROMPT)
            d = pool_prompt(u, TM_PROMPT)
            xp = out_odd(xp, d, w_grp_bf, scale_c[o], w_out_bf,
                                ln_mix_g[layer], ln_mix_b[layer], alpha, TM_PROMPT)
            up = jnp.concatenate([jnp.zeros((bp, buf, D), F32), u.reshape(bp, sp_len, D)], axis=1)
            outs["pp"].append(up[:, -buf:])
            u = matmul(xs, w_in_bf, Ts)
            ext = jnp.concatenate([jnp.zeros((bs, 1, D), F32), state_pool[o], u.reshape(bs, ls, D)], axis=1)
            d = pool_sample(ext, ls, past_len).reshape(Ts, D)
            xs = out_odd(xs, d, w_grp_bf, scale_c[o], w_out_bf,
                                ln_mix_g[layer], ln_mix_b[layer], alpha, Ts)
            outs["ps"].append(ext[:, -buf:])
        xp = _ffn(xp, w_router, router_bias, w_up_bf, w_down_bf, layer,
                  ln_ffn_g[layer], ln_ffn_b[layer], alpha, TM_PROMPT, MOE_TILE_PROMPT)
        xs = _ffn(xs, w_router, router_bias, w_up_bf, w_down_bf, layer,
                  ln_ffn_g[layer], ln_ffn_b[layer], alpha, Ts, MOE_TILE_SAMPLE)

    st = lambda n: jnp.stack(outs[n])
    return (xp.reshape(bp, sp_len, D), xs.reshape(bs, ls, D),
            st("kp"), st("vp"), st("ks"), st("vs"), st("cp"), st("cs"), st("pp"), st("ps"))
```

```python
import functools
import math

import jax
import jax.numpy as jnp
from jax import lax
from jax.experimental import pallas as pl
from jax.experimental.pallas import tpu as pltpu

V7X_LANES = 128
V7X_SUBLANES = 8
V7X_VMEM_BYTES = 64 * 1024 * 1024
VMEM_LIMIT = V7X_VMEM_BYTES - 8 * 1024 * 1024

PAGE = 128
HEAD_DIM = 64
HEADS_PER_BLOCK = V7X_LANES // HEAD_DIM
POOL_WINDOWS = (2, 4, 8, 16)
POOL_PAD = 16
LN_EPS = 1e-5
N_GROUPS = 4
GROUP_SIZE = 4

BF16 = jnp.bfloat16
F32 = jnp.float32


def _params(*sem):
    return pltpu.CompilerParams(dimension_semantics=sem, vmem_limit_bytes=VMEM_LIMIT)


def _dot(a, b):
    return jnp.dot(a, b, preferred_element_type=F32)


def _dot_nt(a, b):
    return lax.dot_general(a, b, (((1,), (1,)), ((), ())), preferred_element_type=F32)


LOG2E = math.log2(math.e)


SOFTPLUS2_CLAMP = 100.0


def _softplus2(z2):
    return jnp.maximum(z2, jnp.log2(1.0 + jnp.exp2(jnp.minimum(z2, SOFTPLUS2_CLAMP))))


def _post_ln(x, sub, g, b, alpha):
    y = alpha * x + sub
    mu = jnp.mean(y, axis=-1, keepdims=True)
    yc = y - mu
    var = jnp.mean(yc * yc, axis=-1, keepdims=True)
    return yc * lax.rsqrt(var + LN_EPS) * g + b


def _store_head_expanded(o_ref, x):
    lane = lax.broadcasted_iota(jnp.int32, (1, V7X_LANES), 1)
    for h in range(x.shape[1] // HEAD_DIM):
        p, hh = divmod(h, HEADS_PER_BLOCK)
        blk = x[:, p * V7X_LANES:(p + 1) * V7X_LANES]
        o_ref[:, h * V7X_LANES:(h + 1) * V7X_LANES] = jnp.where(
            (lane // HEAD_DIM) == hh, blk, 0.0).astype(o_ref.dtype)


def _proj_even_kernel(*refs, cw, seq_len, has_state, expand_heads):
    if has_state:
        (x_ref, w_ref, cwt_ref, p1_ref, p2_ref,
         yc_ref, h_ref, q_ref, k_ref, v_ref, kb_ref, vb_ref, carry_ref) = refs
    else:
        (x_ref, w_ref, cwt_ref,
         yc_ref, h_ref, q_ref, k_ref, v_ref, kb_ref, vb_ref, carry_ref) = refs
    i = pl.program_id(0)
    tm = x_ref.shape[0]
    xb = x_ref[...].astype(BF16)

    def sec(n):
        return _dot(xb, w_ref[:, n * cw:(n + 1) * cw])

    gb, gc, xv = sec(0), sec(1), sec(2)
    h = gc * xv
    hm1 = pltpu.roll(h, 1, axis=0)
    hm2 = pltpu.roll(h, 2, axis=0)
    row = lax.broadcasted_iota(jnp.int32, (tm, 1), 0)
    if has_state:
        l = row % seq_len
        hm1 = jnp.where(l < 1, p1_ref[...], hm1)
        hm2 = jnp.where(l < 2, p2_ref[...], hm2)
    else:
        @pl.when(i == 0)
        def _():
            carry_ref[...] = jnp.zeros_like(carry_ref)
        prev = carry_ref[...]
        last = V7X_SUBLANES - 1
        hm1 = jnp.where(row == 0, prev[last:last + 1, :], hm1)
        hm2 = jnp.where(row == 0, prev[last - 1:last, :], hm2)
        hm2 = jnp.where(row == 1, prev[last:last + 1, :], hm2)
        carry_ref[...] = h[tm - V7X_SUBLANES:, :]
    cwt = cwt_ref[...]
    conv = cwt[0:1, :] * hm2 + cwt[1:2, :] * hm1 + cwt[2:3, :] * h
    yc_ref[...] = (gb * conv).astype(BF16)
    h_ref[...] = h
    q_ref[...] = (sec(3) * (HEAD_DIM ** -0.5 * LOG2E)).astype(BF16)
    k = sec(4)
    v = sec(5)
    k_ref[...] = k
    v_ref[...] = v
    if expand_heads:
        _store_head_expanded(kb_ref, k)
        _store_head_expanded(vb_ref, v)
    else:
        kb_ref[...] = k.astype(BF16)
        vb_ref[...] = v.astype(BF16)


def proj_even(x, w_bf, conv_w, state_rows, seq_len, tm, expand_heads):
    T, D = x.shape
    cw = conv_w.shape[1]
    assert w_bf.shape == (D, 6 * cw) and T % tm == 0
    has_state = state_rows is not None
    if has_state:
        assert T == tm
    bw = cw * HEADS_PER_BLOCK if expand_heads else cw
    row_in = pl.BlockSpec((tm, D), lambda i: (i, 0))
    row_out = pl.BlockSpec((tm, cw), lambda i: (i, 0))
    row_kv = pl.BlockSpec((tm, bw), lambda i: (i, 0))
    in_specs = [row_in,
                pl.BlockSpec((D, 6 * cw), lambda i: (0, 0)),
                pl.BlockSpec((3, cw), lambda i: (0, 0))]
    args = [x, w_bf, conv_w]
    if has_state:
        in_specs += [row_out, row_out]
        args += list(state_rows)
    out_shape = [jax.ShapeDtypeStruct((T, cw), d) for d in (BF16, F32, BF16, F32, F32)]
    out_shape += [jax.ShapeDtypeStruct((T, bw), BF16)] * 2
    return pl.pallas_call(
        functools.partial(_proj_even_kernel, cw=cw, seq_len=seq_len, has_state=has_state,
                          expand_heads=expand_heads),
        grid=(T // tm,),
        in_specs=in_specs,
        out_specs=[row_out] * 5 + [row_kv] * 2,
        out_shape=out_shape,
        scratch_shapes=[pltpu.VMEM((V7X_SUBLANES, cw), F32)],
        compiler_params=_params("arbitrary"),
        name="proj_even",
    )(*args)


def _sb_weights(z, nu, rs, mask):
    tk = nu.shape[0]
    groups = z.shape[1] // tk
    cols = lambda x, g: x[:, g * tk:(g + 1) * tk]
    sp = _softplus2(z)
    if mask is not None:
        mask = jnp.concatenate([mask] * groups, axis=1)
        spm = jnp.where(mask, sp, 0.0)
    else:
        spm = sp
    hi = spm.astype(BF16)
    off = [_dot(cols(hi, g), nu) + rs[g] for g in range(groups)]
    a = jnp.exp2((z - sp) + jnp.concatenate(off, axis=1))
    if mask is not None:
        a = jnp.where(mask, a, 0.0)
    return a.astype(BF16), [rs[g] - jnp.sum(cols(spm, g), axis=1, keepdims=True) for g in range(groups)]


def _sb_prompt_kernel(bias_ref, q_ref, k_ref, v_ref, nu_ref, o_ref, acc_ref, r_ref,
                      z0_ref, z1_ref, a0_ref, a1_ref, *, tk):
    p = pl.program_id(0)
    i = pl.program_id(1)
    tq = q_ref.shape[0]
    nd = tq // tk
    groups = HEADS_PER_BLOCK
    col = lax.broadcasted_iota(jnp.int32, (1, groups * tk), 1)
    bias = jnp.zeros((1, groups * tk), F32)
    for g in range(groups):
        bias = jnp.where(col // tk == g, bias_ref[groups * p + g] * LOG2E, bias)
    acc_ref[...] = jnp.zeros_like(acc_ref)
    r_ref[...] = jnp.zeros_like(r_ref)

    def rows(ref, chunk):
        start = pl.multiple_of(chunk * tk, tk)
        return jnp.concatenate([ref[pl.ds(start, tk), g * V7X_LANES:(g + 1) * V7X_LANES]
                                for g in range(groups)], axis=0)

    def logits(chunk):
        return _dot_nt(q_ref[...], rows(k_ref, chunk)) + bias

    def weights(z, mask):
        a, rs = _sb_weights(z, nu_ref[...], [r_ref[g] for g in range(groups)], mask)
        for g in range(groups):
            r_ref[g] = rs[g]
        return a

    qpos = i * tq + lax.broadcasted_iota(jnp.int32, (tq, 1), 0)
    for d in reversed(range(nd)):
        chunk = i * nd + d
        kpos = chunk * tk + lax.broadcasted_iota(jnp.int32, (1, tk), 1)
        acc_ref[...] += _dot(weights(logits(chunk), kpos < qpos), rows(v_ref, chunk))

    n = i * nd
    z_refs, a_refs = (z0_ref, z1_ref), (a0_ref, a1_ref)
    z0_ref[...] = logits(jnp.maximum(n - 1, 0))
    a1_ref[...] = jnp.zeros_like(a1_ref)

    def stage(cur, chunk):
        nxt = 1 - cur
        z_refs[nxt][...] = logits(jnp.maximum(chunk - 1, 0))
        acc_ref[...] += _dot(a_refs[nxt][...], rows(v_ref, chunk + 1))
        a_refs[cur][...] = weights(z_refs[cur][...], None)

    def body(t, carry):
        for s in range(2):
            stage(s, n - 1 - 2 * t - s)
        return carry

    assert nd % 2 == 0
    lax.fori_loop(0, n // 2, body, 0)
    acc_ref[...] += _dot(a1_ref[...], rows(v_ref, 0))
    o_ref[...] = acc_ref[...].astype(o_ref.dtype)


def _neg_upper(n):
    j = lax.broadcasted_iota(jnp.int32, (n, n), 0)
    s = lax.broadcasted_iota(jnp.int32, (n, n), 1)
    return jnp.where(j > s, -1.0, 0.0).astype(BF16)


def sb_prompt(q_bf, kx_bf, vx_bf, bias, tq, tk):
    T, W = q_bf.shape
    assert T % tq == 0 and tq % tk == 0 and W % V7X_LANES == 0
    assert kx_bf.shape == (T, HEADS_PER_BLOCK * W)
    nblk = W // V7X_LANES
    qspec = pl.BlockSpec((tq, V7X_LANES), lambda p, i: (i, p))
    kvspec = pl.BlockSpec((T, HEADS_PER_BLOCK * V7X_LANES), lambda p, i: (0, p))
    return pl.pallas_call(
        functools.partial(_sb_prompt_kernel, tk=tk),
        grid=(nblk, T // tq),
        in_specs=[pl.BlockSpec(memory_space=pltpu.SMEM), qspec, kvspec, kvspec,
                  pl.BlockSpec((tk, tk), lambda p, i: (0, 0))],
        out_specs=qspec,
        out_shape=jax.ShapeDtypeStruct((T, W), BF16),
        scratch_shapes=[pltpu.VMEM((tq, V7X_LANES), F32),
                        pltpu.VMEM((HEADS_PER_BLOCK, tq, 1), F32),
                        pltpu.VMEM((tq, HEADS_PER_BLOCK * tk), F32),
                        pltpu.VMEM((tq, HEADS_PER_BLOCK * tk), F32),
                        pltpu.VMEM((tq, HEADS_PER_BLOCK * tk), BF16),
                        pltpu.VMEM((tq, HEADS_PER_BLOCK * tk), BF16)],
        compiler_params=_params("arbitrary", "arbitrary"),
        name="sb_prompt",
    )(bias, q_bf, kx_bf, vx_bf, _neg_upper(tk))


def _sb_decode_kernel(pt_ref, qbd_ref, brow_ref, knt_ref, vnt_ref, nu_ref, *rest, pages, n_new, n_heads):
    kt_refs = rest[:pages]
    vt_refs = rest[pages:2 * pages]
    o_ref, acc_ref, r_ref = rest[2 * pages:]
    j = pl.program_id(1)
    qbd = qbd_ref[...]
    brow = brow_ref[...]
    nu = nu_ref[...]
    tk = nu.shape[0]
    rows = qbd.shape[0]

    @pl.when(j == 0)
    def _():
        l = lax.broadcasted_iota(jnp.int32, (rows, 1), 0) // n_heads
        c = lax.broadcasted_iota(jnp.int32, (1, PAGE), 1)
        z = _dot(qbd, knt_ref[...]) + brow
        a, (r,) = _sb_weights(z, nu[:PAGE, :PAGE], [jnp.zeros((rows, 1), F32)], c < l)
        acc_ref[...] = _dot_nt(a, vnt_ref[...])
        r_ref[...] = r

    kt = jnp.concatenate([ref[...].astype(BF16) for ref in kt_refs], axis=1)
    z = _dot(qbd, kt) + brow
    r = r_ref[...]
    parts = []
    for c in reversed(range(pages * PAGE // tk)):
        a, (r,) = _sb_weights(z[:, c * tk:(c + 1) * tk], nu, [r], None)
        parts.append(a)
    vt = jnp.concatenate([ref[...].astype(BF16) for ref in vt_refs], axis=1)
    acc_ref[...] += _dot_nt(jnp.concatenate(parts[::-1], axis=1), vt)
    r_ref[...] = r

    @pl.when(j == pl.num_programs(1) - 1)
    def _():
        acc = acc_ref[...]
        rr = lax.broadcasted_iota(jnp.int32, acc.shape, 0) % n_heads
        cc = lax.broadcasted_iota(jnp.int32, acc.shape, 1) // HEAD_DIM
        accm = jnp.where(rr == cc, acc, 0.0)
        o_ref[...] = jnp.sum(accm.reshape(n_new, n_heads, acc.shape[1]), axis=1).astype(o_ref.dtype)


def sb_decode(q_bf, k_new_bf, v_new_bf, cache_kt, cache_vt, page_table, bias, n_heads, pages, tk):
    B, L, W = q_bf.shape
    n_pages = page_table.shape[1]
    assert n_pages % pages == 0 and W == n_heads * HEAD_DIM and L <= PAGE
    assert (pages * PAGE) % tk == 0 and tk % PAGE == 0
    rows = L * n_heads
    col_head = jnp.arange(W, dtype=jnp.int32) // HEAD_DIM
    keep = (jnp.arange(rows, dtype=jnp.int32) % n_heads)[:, None] == col_head[None, :]
    qbd = jnp.where(keep[None], jnp.repeat(q_bf, n_heads, axis=1), jnp.zeros((), BF16))
    brow = jnp.tile(bias.astype(F32) * LOG2E, L).reshape(rows, 1)
    pad = ((0, 0), (0, PAGE - L), (0, 0))
    knt = jnp.transpose(jnp.pad(k_new_bf, pad), (0, 2, 1))
    vnt = jnp.transpose(jnp.pad(v_new_bf, pad), (0, 2, 1))
    steps = n_pages // pages

    def page_map(t):
        def index(b, j, pt):
            return (pt[b * n_pages + (steps - 1 - j) * pages + t], 0, 0)
        return index

    per_b = lambda shape: pl.BlockSpec((None,) + shape, lambda b, j, pt: (b, 0, 0))
    page_specs = [pl.BlockSpec((None, W, PAGE), page_map(t)) for t in range(pages)]
    grid_spec = pltpu.PrefetchScalarGridSpec(
        num_scalar_prefetch=1,
        grid=(B, steps),
        in_specs=[per_b((rows, W)),
                  pl.BlockSpec((rows, 1), lambda b, j, pt: (0, 0)),
                  per_b((W, PAGE)), per_b((W, PAGE)),
                  pl.BlockSpec((tk, tk), lambda b, j, pt: (0, 0))] + page_specs + page_specs,
        out_specs=per_b((L, W)),
        scratch_shapes=[pltpu.VMEM((rows, W), F32), pltpu.VMEM((rows, 1), F32)],
    )
    return pl.pallas_call(
        functools.partial(_sb_decode_kernel, pages=pages, n_new=L, n_heads=n_heads),
        grid_spec=grid_spec,
        out_shape=jax.ShapeDtypeStruct((B, L, W), F32),
        compiler_params=_params("arbitrary", "arbitrary"),
        name="sb_decode",
    )(page_table.reshape(-1), qbd, brow, knt, vnt, _neg_upper(tk),
      *([cache_kt] * pages), *([cache_vt] * pages))


def _out_even_kernel(x_ref, yc_ref, att_ref, wt_ref, wb_ref, g_ref, b_ref, o_ref, *, alpha):
    m = _dot(yc_ref[...], wt_ref[...]) + _dot(att_ref[...].astype(BF16), wb_ref[...])
    o_ref[...] = _post_ln(x_ref[...], m, g_ref[...], b_ref[...], alpha)


def out_even(x, yc, att, w_out_bf, g, b, alpha, tm):
    T, D = x.shape
    cw = yc.shape[1]
    row = lambda w: pl.BlockSpec((tm, w), lambda i: (i, 0))
    full = lambda a: pl.BlockSpec(a.shape, lambda i: (0,) * a.ndim)
    wt, wb = w_out_bf[:cw], w_out_bf[cw:]
    g2, b2 = g.reshape(1, D), b.reshape(1, D)
    return pl.pallas_call(
        functools.partial(_out_even_kernel, alpha=alpha),
        grid=(T // tm,),
        in_specs=[row(D), row(cw), row(att.shape[1]), full(wt), full(wb), full(g2), full(b2)],
        out_specs=row(D),
        out_shape=jax.ShapeDtypeStruct((T, D), F32),
        compiler_params=_params("parallel"),
        name="out_even",
    )(x, yc, att, wt, wb, g2, b2)


def _router_kernel(x_ref, wrt_ref, rb_ref, before_ref, eid_ref, gate_ref, rank_ref, count_ref, cnt_ref):
    n_exp = wrt_ref.shape[0]
    logits = lax.dot_general(wrt_ref[...], x_ref[...], (((1,), (1,)), ((), ())),
                             precision=lax.Precision.HIGHEST, preferred_element_type=F32)
    s = jax.nn.sigmoid(logits)
    sel = s + rb_ref[...]
    srow = [s[e:e + 1, :] for e in range(n_exp)]
    row = [sel[e:e + 1, :] for e in range(n_exp)]
    best_score, best_g = None, None
    for g in range(N_GROUPS):
        m = row[g * GROUP_SIZE:(g + 1) * GROUP_SIZE]
        score = None
        for a in range(GROUP_SIZE):
            for b in range(a + 1, GROUP_SIZE):
                pair = m[a] + m[b]
                score = pair if score is None else jnp.maximum(score, pair)
        if best_score is None:
            best_score, best_g = score, jnp.zeros_like(score, dtype=jnp.int32)
        else:
            upd = score > best_score
            best_score = jnp.where(upd, score, best_score)
            best_g = jnp.where(upd, g, best_g)
    neg = -jnp.inf
    masked = [jnp.where(best_g == (e // GROUP_SIZE), row[e], neg) for e in range(n_exp)]

    def first_argmax(vals):
        bv, bi = vals[0], jnp.zeros_like(best_g)
        for e in range(1, n_exp):
            upd = vals[e] > bv
            bv = jnp.where(upd, vals[e], bv)
            bi = jnp.where(upd, e, bi)
        return bi

    i1 = first_argmax(masked)
    i2 = first_argmax([jnp.where(i1 == e, neg, masked[e]) for e in range(n_exp)])
    g1 = sum(jnp.where(i1 == e, srow[e], 0.0) for e in range(n_exp))
    g2 = sum(jnp.where(i2 == e, srow[e], 0.0) for e in range(n_exp))
    tot = g1 + g2
    eid_ref[0:1, :] = i1
    eid_ref[1:2, :] = i2
    gate_ref[0:1, :] = g1 / tot
    gate_ref[1:2, :] = g2 / tot

    @pl.when(pl.program_id(0) == 0)
    def _():
        cnt_ref[...] = jnp.zeros_like(cnt_ref)

    tm = x_ref.shape[0]
    eidx = lax.broadcasted_iota(jnp.int32, (n_exp, tm), 0)
    cnt = cnt_ref[...]
    for k, ik in enumerate((i1, i2)):
        hot = eidx == ik
        hot_f = jnp.where(hot, 1.0, 0.0)
        earlier = _dot(hot_f.astype(BF16), before_ref[...])
        rank = jnp.sum(jnp.where(hot, cnt + earlier, 0.0), axis=0, keepdims=True)
        rank_ref[k:k + 1, :] = rank.astype(jnp.int32)
        cnt = cnt + jnp.sum(hot_f, axis=1, keepdims=True)
    cnt_ref[...] = cnt
    count_ref[...] = jnp.broadcast_to(cnt, count_ref.shape).astype(jnp.int32)


def router(x, w_router, router_bias, tm):
    T, D = x.shape
    E = w_router.shape[1]
    s = lax.broadcasted_iota(jnp.int32, (tm, tm), 0)
    t = lax.broadcasted_iota(jnp.int32, (tm, tm), 1)
    before = jnp.where(s < t, 1.0, 0.0).astype(BF16)
    slots = pl.BlockSpec((2, tm), lambda i: (0, i))
    eid, gate, rank, counts = pl.pallas_call(
        _router_kernel,
        grid=(T // tm,),
        in_specs=[pl.BlockSpec((tm, D), lambda i: (i, 0)),
                  pl.BlockSpec((E, D), lambda i: (0, 0)),
                  pl.BlockSpec((E, 1), lambda i: (0, 0)),
                  pl.BlockSpec((tm, tm), lambda i: (0, 0))],
        out_specs=[slots, slots, slots, pl.BlockSpec((E, V7X_LANES), lambda i: (0, 0))],
        out_shape=[jax.ShapeDtypeStruct((2, T), jnp.int32), jax.ShapeDtypeStruct((2, T), F32),
                   jax.ShapeDtypeStruct((2, T), jnp.int32),
                   jax.ShapeDtypeStruct((E, V7X_LANES), jnp.int32)],
        scratch_shapes=[pltpu.VMEM((E, 1), F32)],
        compiler_params=_params("arbitrary"),
        name="router",
    )(x, w_router.T, router_bias.reshape(E, 1).astype(F32), before)
    return eid, gate, rank, counts[:, 0]


def _gelu_tanh(x):
    c = math.sqrt(2.0 / math.pi)
    return 0.5 * x * (1.0 + jnp.tanh(c * (x + 0.044715 * (x * x * x))))


ROW_COPY_UNROLL = 8


def _row_copies(pos_ref, step, tm, make):
    n_tok = pos_ref.shape[0] // 2

    def copy(r, k):
        return make(r, k, pos_ref[k * n_tok + step * tm + r])

    def start(r, c):
        copy(r, 0).start()
        copy(r, 1).start()
        return c

    def wait(r, c):
        copy(r, 0).wait()
        copy(r, 1).wait()
        return c

    lax.fori_loop(0, tm, start, 0, unroll=ROW_COPY_UNROLL)
    lax.fori_loop(0, tm, wait, 0, unroll=ROW_COPY_UNROLL)


def _dispatch_kernel(pos_ref, x_ref, zeros_ref, xs_ref, sem):
    del zeros_ref
    tm = x_ref.shape[0]

    def make(r, k, p):
        return pltpu.make_async_copy(x_ref.at[pl.ds(r, 1), :], xs_ref.at[pl.ds(p, 1), :], sem)

    _row_copies(pos_ref, pl.program_id(0), tm, make)


def _experts_kernel(te_ref, used_ref, xs_ref, wu_ref, wd_ref, ys_ref):
    del te_ref
    j = pl.program_id(0)

    @pl.when(j < used_ref[0])
    def _():
        h = _gelu_tanh(_dot(xs_ref[...].astype(BF16), wu_ref[...]))
        ys_ref[...] = _dot(h.astype(BF16), wd_ref[...])

    @pl.when(j >= used_ref[0])
    def _():
        ys_ref[...] = jnp.zeros_like(ys_ref)


def _combine_kernel(pos_ref, x_ref, gate_ref, ys_ref, g_ref, b_ref, o_ref, buf_ref, sem, *, alpha):
    tm = x_ref.shape[0]

    def make(r, k, p):
        return pltpu.make_async_copy(ys_ref.at[pl.ds(p, 1), :], buf_ref.at[k, pl.ds(r, 1), :], sem)

    _row_copies(pos_ref, pl.program_id(0), tm, make)
    gate = gate_ref[...]
    y = gate[:, 0:1] * buf_ref[0] + gate[:, 1:2] * buf_ref[1]
    o_ref[...] = _post_ln(x_ref[...], y, g_ref[...], b_ref[...], alpha)


def _moe_plan(eid, rank, counts, tile, n_tiles):
    padded = (counts + tile - 1) // tile * tile
    ends = jnp.cumsum(padded)
    starts = ends - padded
    pos = rank + sum(jnp.where(eid == e, starts[e], 0) for e in range(counts.shape[0]))
    tile_ends = ends // tile
    used = tile_ends[-1]
    j = jnp.arange(n_tiles, dtype=jnp.int32)
    tile_expert = jnp.sum(tile_ends[None, :] <= jnp.minimum(j, used - 1)[:, None], axis=1)
    return pos.astype(jnp.int32), tile_expert.astype(jnp.int32), used.astype(jnp.int32).reshape(1)


def _dma_params():
    return pltpu.CompilerParams(dimension_semantics=("arbitrary",), vmem_limit_bytes=VMEM_LIMIT,
                                disable_bounds_checks=True)


def moe_sparse(x, eid, gate, rank, counts, w_up_bf, w_down_bf, layer, g, b, alpha, tm, tile):
    T, D = x.shape
    _, E, _, F = w_up_bf.shape
    n_tiles = (2 * T + tile - 1) // tile + E
    pos, tile_expert, used = _moe_plan(eid, rank, counts, tile, n_tiles)
    pos_flat = pos.reshape(-1)
    any_spec = pl.BlockSpec(memory_space=pl.ANY)

    xs = pl.pallas_call(
        _dispatch_kernel,
        grid_spec=pltpu.PrefetchScalarGridSpec(
            num_scalar_prefetch=1,
            grid=(T // tm,),
            in_specs=[pl.BlockSpec((tm, D), lambda i, pos: (i, 0)), any_spec],
            out_specs=any_spec,
            scratch_shapes=[pltpu.SemaphoreType.DMA(())],
        ),
        out_shape=jax.ShapeDtypeStruct((n_tiles * tile, D), F32),
        input_output_aliases={2: 0},
        compiler_params=_dma_params(),
        name="moe_dispatch",
    )(pos_flat, x, jnp.zeros((n_tiles * tile, D), F32))

    ys = pl.pallas_call(
        _experts_kernel,
        grid_spec=pltpu.PrefetchScalarGridSpec(
            num_scalar_prefetch=2,
            grid=(n_tiles,),
            in_specs=[pl.BlockSpec((tile, D), lambda j, te, used: (jnp.minimum(j, used[0] - 1), 0)),
                      pl.BlockSpec((None, None, D, F), lambda j, te, used: (layer, te[j], 0, 0)),
                      pl.BlockSpec((None, None, F, D), lambda j, te, used: (layer, te[j], 0, 0))],
            out_specs=pl.BlockSpec((tile, D), lambda j, te, used: (j, 0)),
        ),
        out_shape=jax.ShapeDtypeStruct((n_tiles * tile, D), F32),
        compiler_params=_params("arbitrary"),
        name="moe_experts",
    )(tile_expert, used, xs, w_up_bf, w_down_bf)

    vec = pl.BlockSpec((1, D), lambda i, pos: (0, 0))
    row = pl.BlockSpec((tm, D), lambda i, pos: (i, 0))
    return pl.pallas_call(
        functools.partial(_combine_kernel, alpha=alpha),
        grid_spec=pltpu.PrefetchScalarGridSpec(
            num_scalar_prefetch=1,
            grid=(T // tm,),
            in_specs=[row, pl.BlockSpec((tm, 2), lambda i, pos: (i, 0)), any_spec, vec, vec],
            out_specs=row,
            scratch_shapes=[pltpu.VMEM((2, tm, D), F32), pltpu.SemaphoreType.DMA(())],
        ),
        out_shape=jax.ShapeDtypeStruct((T, D), F32),
        compiler_params=_dma_params(),
        name="moe_combine",
    )(pos_flat, x, gate.T, ys, g.reshape(1, D), b.reshape(1, D))


def _matmul_kernel(x_ref, w_ref, o_ref):
    o_ref[...] = _dot(x_ref[...].astype(BF16), w_ref[...])


def matmul(x, w_bf, tm):
    T, K = x.shape
    N = w_bf.shape[1]
    return pl.pallas_call(
        _matmul_kernel,
        grid=(T // tm,),
        in_specs=[pl.BlockSpec((tm, K), lambda i: (i, 0)), pl.BlockSpec((K, N), lambda i: (0, 0))],
        out_specs=pl.BlockSpec((tm, N), lambda i: (i, 0)),
        out_shape=jax.ShapeDtypeStruct((T, N), F32),
        compiler_params=_params("parallel"),
        name="matmul",
    )(x, w_bf)


def _pool_windows(ext_ref, tm, pos0, o_ref):
    C = ext_ref.shape[1]
    gw = C // len(POOL_WINDOWS)
    pos = pos0 + lax.broadcasted_iota(jnp.int32, (tm, 1), 0)
    for g, w in enumerate(POOL_WINDOWS):
        cols = slice(g * gw, (g + 1) * gw)
        cur = ext_ref[pl.ds(POOL_PAD, tm), cols]
        tot = cur
        for d in range(1, w):
            tot = tot + ext_ref[pl.ds(POOL_PAD - d, tm), cols]
        cnt = jnp.minimum(pos + 1, w).astype(F32)
        o_ref[:, cols] = (tot / cnt - cur).astype(o_ref.dtype)


def _pool_prompt_kernel(u_ref, o_ref, ext_ref):
    i = pl.program_id(0)
    tm = u_ref.shape[0]

    @pl.when(i == 0)
    def _():
        ext_ref[pl.ds(0, POOL_PAD), :] = jnp.zeros((POOL_PAD, ext_ref.shape[1]), F32)

    ext_ref[pl.ds(POOL_PAD, tm), :] = u_ref[...]
    _pool_windows(ext_ref, tm, i * tm, o_ref)
    ext_ref[pl.ds(0, POOL_PAD), :] = ext_ref[pl.ds(tm, POOL_PAD), :]


def pool_prompt(u, tm):
    T, C = u.shape
    spec = pl.BlockSpec((tm, C), lambda i: (i, 0))
    return pl.pallas_call(
        _pool_prompt_kernel,
        grid=(T // tm,),
        in_specs=[spec],
        out_specs=spec,
        out_shape=jax.ShapeDtypeStruct((T, C), BF16),
        scratch_shapes=[pltpu.VMEM((POOL_PAD + tm, C), F32)],
        compiler_params=_params("arbitrary"),
        name="pool_prompt",
    )(u)


def _pool_sample_kernel(ext_ref, o_ref, *, n_new, start_pos):
    _pool_windows(ext_ref, n_new, start_pos, o_ref)


def pool_sample(ext, n_new, start_pos):
    B, R, C = ext.shape
    assert R == POOL_PAD + n_new
    return pl.pallas_call(
        functools.partial(_pool_sample_kernel, n_new=n_new, start_pos=start_pos),
        grid=(B,),
        in_specs=[pl.BlockSpec((None, R, C), lambda b: (b, 0, 0))],
        out_specs=pl.BlockSpec((None, n_new, C), lambda b: (b, 0, 0)),
        out_shape=jax.ShapeDtypeStruct((B, n_new, C), F32),
        compiler_params=_params("parallel"),
        name="pool_sample",
    )(ext)


def _out_odd_kernel(x_ref, d_ref, wg_ref, sc_ref, wo_ref, g_ref, b_ref, o_ref, *, alpha):
    d = d_ref[...].astype(BF16)
    ng, gw, _ = wg_ref.shape
    ys = [_dot(d[:, g * gw:(g + 1) * gw], wg_ref[g]) for g in range(ng)]
    y = (jnp.concatenate(ys, axis=1) * sc_ref[...]).astype(BF16)
    o_ref[...] = _post_ln(x_ref[...], _dot(y, wo_ref[...]), g_ref[...], b_ref[...], alpha)


def out_odd(x, d, w_grp_bf, scale, w_out_bf, g, b, alpha, tm):
    T, D = x.shape
    row = pl.BlockSpec((tm, D), lambda i: (i, 0))
    full = lambda a: pl.BlockSpec(a.shape, lambda i: (0,) * a.ndim)
    sc, g2, b2 = scale.reshape(1, D), g.reshape(1, D), b.reshape(1, D)
    return pl.pallas_call(
        functools.partial(_out_odd_kernel, alpha=alpha),
        grid=(T // tm,),
        in_specs=[row, row, full(w_grp_bf), full(sc), full(w_out_bf), full(g2), full(b2)],
        out_specs=row,
        out_shape=jax.ShapeDtypeStruct((T, D), F32),
        compiler_params=_params("parallel"),
        name="out_odd",
    )(x, d, w_grp_bf, sc, w_out_bf, g2, b2)


TM_PROMPT = 512
TQ_PROMPT = 512
TK_PROMPT = 256
DECODE_PAGES = 16
DECODE_TK = 256
MOE_TILE_PROMPT = 512
MOE_TILE_SAMPLE = 128


def _ffn(x, w_router, router_bias, w_up_bf, w_down_bf, layer, g, b, alpha, tm, tile):
    eid, gate, rank, counts = router(x, w_router, router_bias, tm)
    return moe_sparse(x, eid, gate, rank, counts, w_up_bf, w_down_bf, layer, g, b, alpha, tm, tile)


def kernel(x_prompt, x_sample, cache_k, cache_v, state_conv, state_pool, page_table, w_in_ab, conv_w, sb_bias, w_out_ab, w_in_c, w_grp_c, scale_c, w_out_c, w_router, router_bias, w_exp_up, w_exp_down, ln_mix_g, ln_mix_b, ln_ffn_g, ln_ffn_b):
    bp, sp_len, D = x_prompt.shape
    bs, ls, _ = x_sample.shape
    assert bp == 1
    depth = w_exp_up.shape[0]
    n_heads = sb_bias.shape[1]
    cw = conv_w.shape[2]
    aw = n_heads * HEAD_DIM
    n_pages = page_table.shape[1]
    past_len = n_pages * PAGE
    alpha = (2 * depth) ** 0.25
    Tp, Ts = bp * sp_len, bs * ls

    xp = x_prompt.reshape(Tp, D)
    xs = x_sample.reshape(Ts, D)
    outs = {n: [] for n in ("kp", "vp", "ks", "vs", "cp", "cs", "pp", "ps")}
    cache_kt = jnp.transpose(cache_k.reshape(-1, PAGE, aw), (0, 2, 1))
    cache_vt = jnp.transpose(cache_v.reshape(-1, PAGE, aw), (0, 2, 1))

    w_up_bf = w_exp_up.astype(BF16)
    w_down_bf = w_exp_down.astype(BF16)
    for layer in range(depth):
        if layer % 2 == 0:
            e = layer // 2
            w_in_bf = w_in_ab[e].astype(BF16)
            w_out_bf = w_out_ab[e].astype(BF16)
            yc, h, q, k, v, kb, vb = proj_even(xp, w_in_bf, conv_w[e], None, sp_len, TM_PROMPT, True)
            att = sb_prompt(q, kb, vb, sb_bias[e], TQ_PROMPT, TK_PROMPT)
            xp = out_even(xp, yc, att, w_out_bf, ln_mix_g[layer], ln_mix_b[layer], alpha, TM_PROMPT)
            outs["kp"].append(k.reshape(bp, sp_len, n_heads, HEAD_DIM))
            outs["vp"].append(v.reshape(bp, sp_len, n_heads, HEAD_DIM))
            outs["cp"].append(h.reshape(bp, sp_len, cw)[:, sp_len - 2:])
            st = state_conv[e]
            zero = jnp.zeros((bs, 1, cw), F32)
            p1 = jnp.concatenate([st[:, 1:2]] + [zero] * (ls - 1), axis=1).reshape(Ts, cw)
            p2 = jnp.concatenate([st[:, 0:1], st[:, 1:2]] + [zero] * (ls - 2), axis=1).reshape(Ts, cw)
            yc, h, q, k, v, kb, vb = proj_even(xs, w_in_bf, conv_w[e], (p1, p2), ls, Ts, False)
            att = sb_decode(q.reshape(bs, ls, aw), kb.reshape(bs, ls, aw), vb.reshape(bs, ls, aw),
                            cache_kt, cache_vt, page_table + e * cache_k.shape[1], sb_bias[e], n_heads,
                            DECODE_PAGES, DECODE_TK).reshape(Ts, aw)
            xs = out_even(xs, yc, att, w_out_bf, ln_mix_g[layer], ln_mix_b[layer], alpha, Ts)
            outs["ks"].append(k.reshape(bs, ls, n_heads, HEAD_DIM))
            outs["vs"].append(v.reshape(bs, ls, n_heads, HEAD_DIM))
            hs = jnp.concatenate([st, h.reshape(bs, ls, cw)], axis=1)
            outs["cs"].append(hs[:, -2:])
        else:
            o = layer // 2
            w_in_bf = w_in_c[o].astype(BF16)
            w_grp_bf = w_grp_c[o].astype(BF16)
            w_out_bf = w_out_c[o].astype(BF16)
            buf = POOL_PAD - 1
            u = matmul(xp, w_in_bf, TM_PROMPT)
            d = pool_prompt(u, TM_PROMPT)
            xp = out_odd(xp, d, w_grp_bf, scale_c[o], w_out_bf,
                                ln_mix_g[layer], ln_mix_b[layer], alpha, TM_PROMPT)
            up = jnp.concatenate([jnp.zeros((bp, buf, D), F32), u.reshape(bp, sp_len, D)], axis=1)
            outs["pp"].append(up[:, -buf:])
            u = matmul(xs, w_in_bf, Ts)
            ext = jnp.concatenate([jnp.zeros((bs, 1, D), F32), state_pool[o], u.reshape(bs, ls, D)], axis=1)
            d = pool_sample(ext, ls, past_len).reshape(Ts, D)
            xs = out_odd(xs, d, w_grp_bf, scale_c[o], w_out_bf,
                                ln_mix_g[layer], ln_mix_b[layer], alpha, Ts)
            outs["ps"].append(ext[:, -buf:])
        xp = _ffn(xp, w_router, router_bias, w_up_bf, w_down_bf, layer,
                  ln_ffn_g[layer], ln_ffn_b[layer], alpha, TM_PROMPT, MOE_TILE_PROMPT)
        xs = _ffn(xs, w_router, router_bias, w_up_bf, w_down_bf, layer,
                  ln_ffn_g[layer], ln_ffn_b[layer], alpha, Ts, MOE_TILE_SAMPLE)

    st = lambda n: jnp.stack(outs[n])
    return (xp.reshape(bp, sp_len, D), xs.reshape(bs, ls, D),
            st("kp"), st("vp"), st("ks"), st("vs"), st("cp"), st("cs"), st("pp"), st("ps"))
```

```python
import functools
import math

import jax
import jax.numpy as jnp
from jax import lax
from jax.experimental import pallas as pl
from jax.experimental.pallas import tpu as pltpu

V7X_LANES = 128
V7X_SUBLANES = 8
V7X_VMEM_BYTES = 64 * 1024 * 1024
VMEM_LIMIT = V7X_VMEM_BYTES - 8 * 1024 * 1024

PAGE = 128
HEAD_DIM = 64
HEADS_PER_BLOCK = V7X_LANES // HEAD_DIM
POOL_WINDOWS = (2, 4, 8, 16)
POOL_PAD = 16
LN_EPS = 1e-5
N_GROUPS = 4
GROUP_SIZE = 4

BF16 = jnp.bfloat16
F32 = jnp.float32


def _params(*sem):
    return pltpu.CompilerParams(dimension_semantics=sem, vmem_limit_bytes=VMEM_LIMIT)


def _dot(a, b):
    return jnp.dot(a, b, preferred_element_type=F32)


def _dot_nt(a, b):
    return lax.dot_general(a, b, (((1,), (1,)), ((), ())), preferred_element_type=F32)


LOG2E = math.log2(math.e)


SOFTPLUS2_CLAMP = 100.0


def _softplus2(z2):
    return jnp.maximum(z2, jnp.log2(1.0 + jnp.exp2(jnp.minimum(z2, SOFTPLUS2_CLAMP))))


def _post_ln(x, sub, g, b, alpha):
    y = alpha * x + sub
    mu = jnp.mean(y, axis=-1, keepdims=True)
    yc = y - mu
    var = jnp.mean(yc * yc, axis=-1, keepdims=True)
    return yc * lax.rsqrt(var + LN_EPS) * g + b


def _store_head_expanded(o_ref, x):
    lane = lax.broadcasted_iota(jnp.int32, (1, V7X_LANES), 1)
    for h in range(x.shape[1] // HEAD_DIM):
        p, hh = divmod(h, HEADS_PER_BLOCK)
        blk = x[:, p * V7X_LANES:(p + 1) * V7X_LANES]
        o_ref[:, h * V7X_LANES:(h + 1) * V7X_LANES] = jnp.where(
            (lane // HEAD_DIM) == hh, blk, 0.0).astype(o_ref.dtype)


def _proj_even_kernel(*refs, cw, seq_len, has_state, expand_heads):
    if has_state:
        (x_ref, w_ref, cwt_ref, p1_ref, p2_ref,
         yc_ref, h_ref, q_ref, k_ref, v_ref, kb_ref, vb_ref, carry_ref) = refs
    else:
        (x_ref, w_ref, cwt_ref,
         yc_ref, h_ref, q_ref, k_ref, v_ref, kb_ref, vb_ref, carry_ref) = refs
    i = pl.program_id(0)
    tm = x_ref.shape[0]
    xb = x_ref[...].astype(BF16)

    def sec(n):
        return _dot(xb, w_ref[:, n * cw:(n + 1) * cw])

    gb, gc, xv = sec(0), sec(1), sec(2)
    h = gc * xv
    hm1 = pltpu.roll(h, 1, axis=0)
    hm2 = pltpu.roll(h, 2, axis=0)
    row = lax.broadcasted_iota(jnp.int32, (tm, 1), 0)
    if has_state:
        l = row % seq_len
        hm1 = jnp.where(l < 1, p1_ref[...], hm1)
        hm2 = jnp.where(l < 2, p2_ref[...], hm2)
    else:
        @pl.when(i == 0)
        def _():
            carry_ref[...] = jnp.zeros_like(carry_ref)
        prev = carry_ref[...]
        last = V7X_SUBLANES - 1
        hm1 = jnp.where(row == 0, prev[last:last + 1, :], hm1)
        hm2 = jnp.where(row == 0, prev[last - 1:last, :], hm2)
        hm2 = jnp.where(row == 1, prev[last:last + 1, :], hm2)
        carry_ref[...] = h[tm - V7X_SUBLANES:, :]
    cwt = cwt_ref[...]
    conv = cwt[0:1, :] * hm2 + cwt[1:2, :] * hm1 + cwt[2:3, :] * h
    yc_ref[...] = (gb * conv).astype(BF16)
    h_ref[...] = h
    q_ref[...] = (sec(3) * (HEAD_DIM ** -0.5 * LOG2E)).astype(BF16)
    k = sec(4)
    v = sec(5)
    k_ref[...] = k
    v_ref[...] = v
    if expand_heads:
        _store_head_expanded(kb_ref, k)
        _store_head_expanded(vb_ref, v)
    else:
        kb_ref[...] = k.astype(BF16)
        vb_ref[...] = v.astype(BF16)


def proj_even(x, w_bf, conv_w, state_rows, seq_len, tm, expand_heads):
    T, D = x.shape
    cw = conv_w.shape[1]
    assert w_bf.shape == (D, 6 * cw) and T % tm == 0
    has_state = state_rows is not None
    if has_state:
        assert T == tm
    bw = cw * HEADS_PER_BLOCK if expand_heads else cw
    row_in = pl.BlockSpec((tm, D), lambda i: (i, 0))
    row_out = pl.BlockSpec((tm, cw), lambda i: (i, 0))
    row_kv = pl.BlockSpec((tm, bw), lambda i: (i, 0))
    in_specs = [row_in,
                pl.BlockSpec((D, 6 * cw), lambda i: (0, 0)),
                pl.BlockSpec((3, cw), lambda i: (0, 0))]
    args = [x, w_bf, conv_w]
    if has_state:
        in_specs += [row_out, row_out]
        args += list(state_rows)
    out_shape = [jax.ShapeDtypeStruct((T, cw), d) for d in (BF16, F32, BF16, F32, F32)]
    out_shape += [jax.ShapeDtypeStruct((T, bw), BF16)] * 2
    return pl.pallas_call(
        functools.partial(_proj_even_kernel, cw=cw, seq_len=seq_len, has_state=has_state,
                          expand_heads=expand_heads),
        grid=(T // tm,),
        in_specs=in_specs,
        out_specs=[row_out] * 5 + [row_kv] * 2,
        out_shape=out_shape,
        scratch_shapes=[pltpu.VMEM((V7X_SUBLANES, cw), F32)],
        compiler_params=_params("arbitrary"),
        name="proj_even",
    )(*args)


def _sb_weights(z, nu, rs, mask):
    tk = nu.shape[0]
    groups = z.shape[1] // tk
    cols = lambda x, g: x[:, g * tk:(g + 1) * tk]
    sp = _softplus2(z)
    if mask is not None:
        mask = jnp.concatenate([mask] * groups, axis=1)
        spm = jnp.where(mask, sp, 0.0)
    else:
        spm = sp
    hi = spm.astype(BF16)
    off = [_dot(cols(hi, g), nu) + rs[g] for g in range(groups)]
    a = jnp.exp2((z - sp) + jnp.concatenate(off, axis=1))
    if mask is not None:
        a = jnp.where(mask, a, 0.0)
    return a.astype(BF16), [rs[g] - jnp.sum(cols(spm, g), axis=1, keepdims=True) for g in range(groups)]


def _sb_prompt_kernel(bias_ref, q_ref, k_ref, v_ref, nu_ref, o_ref, acc_ref, r_ref,
                      z0_ref, z1_ref, a0_ref, a1_ref, *, tk):
    p = pl.program_id(0)
    i = pl.program_id(1)
    tq = q_ref.shape[0]
    nd = tq // tk
    groups = HEADS_PER_BLOCK
    col = lax.broadcasted_iota(jnp.int32, (1, groups * tk), 1)
    bias = jnp.zeros((1, groups * tk), F32)
    for g in range(groups):
        bias = jnp.where(col // tk == g, bias_ref[groups * p + g] * LOG2E, bias)
    acc_ref[...] = jnp.zeros_like(acc_ref)
    r_ref[...] = jnp.zeros_like(r_ref)

    def rows(ref, chunk):
        start = pl.multiple_of(chunk * tk, tk)
        return jnp.concatenate([ref[pl.ds(start, tk), g * V7X_LANES:(g + 1) * V7X_LANES]
                                for g in range(groups)], axis=0)

    def logits(chunk):
        return _dot_nt(q_ref[...], rows(k_ref, chunk)) + bias

    def weights(z, mask):
        a, rs = _sb_weights(z, nu_ref[...], [r_ref[g] for g in range(groups)], mask)
        for g in range(groups):
            r_ref[g] = rs[g]
        return a

    qpos = i * tq + lax.broadcasted_iota(jnp.int32, (tq, 1), 0)
    for d in reversed(range(nd)):
        chunk = i * nd + d
        kpos = chunk * tk + lax.broadcasted_iota(jnp.int32, (1, tk), 1)
        acc_ref[...] += _dot(weights(logits(chunk), kpos < qpos), rows(v_ref, chunk))

    n = i * nd
    z_refs, a_refs = (z0_ref, z1_ref), (a0_ref, a1_ref)
    z0_ref[...] = logits(jnp.maximum(n - 1, 0))
    a1_ref[...] = jnp.zeros_like(a1_ref)

    def stage(cur, chunk):
        nxt = 1 - cur
        z_refs[nxt][...] = logits(jnp.maximum(chunk - 1, 0))
        acc_ref[...] += _dot(a_refs[nxt][...], rows(v_ref, chunk + 1))
        a_refs[cur][...] = weights(z_refs[cur][...], None)

    def body(t, carry):
        for s in range(2):
            stage(s, n - 1 - 2 * t - s)
        return carry

    assert nd % 2 == 0
    lax.fori_loop(0, n // 2, body, 0)
    acc_ref[...] += _dot(a1_ref[...], rows(v_ref, 0))
    o_ref[...] = acc_ref[...].astype(o_ref.dtype)


def _neg_upper(n):
    j = lax.broadcasted_iota(jnp.int32, (n, n), 0)
    s = lax.broadcasted_iota(jnp.int32, (n, n), 1)
    return jnp.where(j > s, -1.0, 0.0).astype(BF16)


def sb_prompt(q_bf, kx_bf, vx_bf, bias, tq, tk):
    T, W = q_bf.shape
    assert T % tq == 0 and tq % tk == 0 and W % V7X_LANES == 0
    assert kx_bf.shape == (T, HEADS_PER_BLOCK * W)
    nblk = W // V7X_LANES
    qspec = pl.BlockSpec((tq, V7X_LANES), lambda p, i: (i, p))
    kvspec = pl.BlockSpec((T, HEADS_PER_BLOCK * V7X_LANES), lambda p, i: (0, p))
    return pl.pallas_call(
        functools.partial(_sb_prompt_kernel, tk=tk),
        grid=(nblk, T // tq),
        in_specs=[pl.BlockSpec(memory_space=pltpu.SMEM), qspec, kvspec, kvspec,
                  pl.BlockSpec((tk, tk), lambda p, i: (0, 0))],
        out_specs=qspec,
        out_shape=jax.ShapeDtypeStruct((T, W), BF16),
        scratch_shapes=[pltpu.VMEM((tq, V7X_LANES), F32),
                        pltpu.VMEM((HEADS_PER_BLOCK, tq, 1), F32),
                        pltpu.VMEM((tq, HEADS_PER_BLOCK * tk), F32),
                        pltpu.VMEM((tq, HEADS_PER_BLOCK * tk), F32),
                        pltpu.VMEM((tq, HEADS_PER_BLOCK * tk), BF16),
                        pltpu.VMEM((tq, HEADS_PER_BLOCK * tk), BF16)],
        compiler_params=_params("arbitrary", "arbitrary"),
        name="sb_prompt",
    )(bias, q_bf, kx_bf, vx_bf, _neg_upper(tk))


def _sb_decode_kernel(pt_ref, qbd_ref, brow_ref, knt_ref, vnt_ref, nu_ref, *rest, pages, n_new, n_heads):
    kt_refs = rest[:pages]
    vt_refs = rest[pages:2 * pages]
    o_ref, acc_ref, r_ref = rest[2 * pages:]
    j = pl.program_id(1)
    qbd = qbd_ref[...]
    brow = brow_ref[...]
    nu = nu_ref[...]
    tk = nu.shape[0]
    rows = qbd.shape[0]

    @pl.when(j == 0)
    def _():
        l = lax.broadcasted_iota(jnp.int32, (rows, 1), 0) // n_heads
        c = lax.broadcasted_iota(jnp.int32, (1, PAGE), 1)
        z = _dot(qbd, knt_ref[...]) + brow
        a, (r,) = _sb_weights(z, nu[:PAGE, :PAGE], [jnp.zeros((rows, 1), F32)], c < l)
        acc_ref[...] = _dot_nt(a, vnt_ref[...])
        r_ref[...] = r

    kt = jnp.concatenate([ref[...].astype(BF16) for ref in kt_refs], axis=1)
    z = _dot(qbd, kt) + brow
    r = r_ref[...]
    parts = []
    for c in reversed(range(pages * PAGE // tk)):
        a, (r,) = _sb_weights(z[:, c * tk:(c + 1) * tk], nu, [r], None)
        parts.append(a)
    vt = jnp.concatenate([ref[...].astype(BF16) for ref in vt_refs], axis=1)
    acc_ref[...] += _dot_nt(jnp.concatenate(parts[::-1], axis=1), vt)
    r_ref[...] = r

    @pl.when(j == pl.num_programs(1) - 1)
    def _():
        acc = acc_ref[...]
        rr = lax.broadcasted_iota(jnp.int32, acc.shape, 0) % n_heads
        cc = lax.broadcasted_iota(jnp.int32, acc.shape, 1) // HEAD_DIM
        accm = jnp.where(rr == cc, acc, 0.0)
        o_ref[...] = jnp.sum(accm.reshape(n_new, n_heads, acc.shape[1]), axis=1).astype(o_ref.dtype)


def sb_decode(q_bf, k_new_bf, v_new_bf, cache_kt, cache_vt, page_table, bias, n_heads, pages, tk):
    B, L, W = q_bf.shape
    n_pages = page_table.shape[1]
    assert n_pages % pages == 0 and W == n_heads * HEAD_DIM and L <= PAGE
    assert (pages * PAGE) % tk == 0 and tk % PAGE == 0
    rows = L * n_heads
    col_head = jnp.arange(W, dtype=jnp.int32) // HEAD_DIM
    keep = (jnp.arange(rows, dtype=jnp.int32) % n_heads)[:, None] == col_head[None, :]
    qbd = jnp.where(keep[None], jnp.repeat(q_bf, n_heads, axis=1), jnp.zeros((), BF16))
    brow = jnp.tile(bias.astype(F32) * LOG2E, L).reshape(rows, 1)
    pad = ((0, 0), (0, PAGE - L), (0, 0))
    knt = jnp.transpose(jnp.pad(k_new_bf, pad), (0, 2, 1))
    vnt = jnp.transpose(jnp.pad(v_new_bf, pad), (0, 2, 1))
    steps = n_pages // pages

    def page_map(t):
        def index(b, j, pt):
            return (pt[b * n_pages + (steps - 1 - j) * pages + t], 0, 0)
        return index

    per_b = lambda shape: pl.BlockSpec((None,) + shape, lambda b, j, pt: (b, 0, 0))
    page_specs = [pl.BlockSpec((None, W, PAGE), page_map(t)) for t in range(pages)]
    grid_spec = pltpu.PrefetchScalarGridSpec(
        num_scalar_prefetch=1,
        grid=(B, steps),
        in_specs=[per_b((rows, W)),
                  pl.BlockSpec((rows, 1), lambda b, j, pt: (0, 0)),
                  per_b((W, PAGE)), per_b((W, PAGE)),
                  pl.BlockSpec((tk, tk), lambda b, j, pt: (0, 0))] + page_specs + page_specs,
        out_specs=per_b((L, W)),
        scratch_shapes=[pltpu.VMEM((rows, W), F32), pltpu.VMEM((rows, 1), F32)],
    )
    return pl.pallas_call(
        functools.partial(_sb_decode_kernel, pages=pages, n_new=L, n_heads=n_heads),
        grid_spec=grid_spec,
        out_shape=jax.ShapeDtypeStruct((B, L, W), F32),
        compiler_params=_params("arbitrary", "arbitrary"),
        name="sb_decode",
    )(page_table.reshape(-1), qbd, brow, knt, vnt, _neg_upper(tk),
      *([cache_kt] * pages), *([cache_vt] * pages))


def _out_even_kernel(x_ref, yc_ref, att_ref, wt_ref, wb_ref, g_ref, b_ref, o_ref, *, alpha):
    m = _dot(yc_ref[...], wt_ref[...]) + _dot(att_ref[...].astype(BF16), wb_ref[...])
    o_ref[...] = _post_ln(x_ref[...], m, g_ref[...], b_ref[...], alpha)


def out_even(x, yc, att, w_out_bf, g, b, alpha, tm):
    T, D = x.shape
    cw = yc.shape[1]
    row = lambda w: pl.BlockSpec((tm, w), lambda i: (i, 0))
    full = lambda a: pl.BlockSpec(a.shape, lambda i: (0,) * a.ndim)
    wt, wb = w_out_bf[:cw], w_out_bf[cw:]
    g2, b2 = g.reshape(1, D), b.reshape(1, D)
    return pl.pallas_call(
        functools.partial(_out_even_kernel, alpha=alpha),
        grid=(T // tm,),
        in_specs=[row(D), row(cw), row(att.shape[1]), full(wt), full(wb), full(g2), full(b2)],
        out_specs=row(D),
        out_shape=jax.ShapeDtypeStruct((T, D), F32),
        compiler_params=_params("parallel"),
        name="out_even",
    )(x, yc, att, wt, wb, g2, b2)


def _router_kernel(x_ref, wrt_ref, rb_ref, before_ref, eid_ref, gate_ref, rank_ref, count_ref, cnt_ref):
    n_exp = wrt_ref.shape[0]
    logits = lax.dot_general(wrt_ref[...], x_ref[...], (((1,), (1,)), ((), ())),
                             precision=lax.Precision.HIGHEST, preferred_element_type=F32)
    s = jax.nn.sigmoid(logits)
    sel = s + rb_ref[...]
    srow = [s[e:e + 1, :] for e in range(n_exp)]
    row = [sel[e:e + 1, :] for e in range(n_exp)]
    best_score, best_g = None, None
    for g in range(N_GROUPS):
        m = row[g * GROUP_SIZE:(g + 1) * GROUP_SIZE]
        score = None
        for a in range(GROUP_SIZE):
            for b in range(a + 1, GROUP_SIZE):
                pair = m[a] + m[b]
                score = pair if score is None else jnp.maximum(score, pair)
        if best_score is None:
            best_score, best_g = score, jnp.zeros_like(score, dtype=jnp.int32)
        else:
            upd = score > best_score
            best_score = jnp.where(upd, score, best_score)
            best_g = jnp.where(upd, g, best_g)
    neg = -jnp.inf
    masked = [jnp.where(best_g == (e // GROUP_SIZE), row[e], neg) for e in range(n_exp)]

    def first_argmax(vals):
        bv, bi = vals[0], jnp.zeros_like(best_g)
        for e in range(1, n_exp):
            upd = vals[e] > bv
            bv = jnp.where(upd, vals[e], bv)
            bi = jnp.where(upd, e, bi)
        return bi

    i1 = first_argmax(masked)
    i2 = first_argmax([jnp.where(i1 == e, neg, masked[e]) for e in range(n_exp)])
    g1 = sum(jnp.where(i1 == e, srow[e], 0.0) for e in range(n_exp))
    g2 = sum(jnp.where(i2 == e, srow[e], 0.0) for e in range(n_exp))
    tot = g1 + g2
    eid_ref[0:1, :] = i1
    eid_ref[1:2, :] = i2
    gate_ref[0:1, :] = g1 / tot
    gate_ref[1:2, :] = g2 / tot

    @pl.when(pl.program_id(0) == 0)
    def _():
        cnt_ref[...] = jnp.zeros_like(cnt_ref)

    tm = x_ref.shape[0]
    eidx = lax.broadcasted_iota(jnp.int32, (n_exp, tm), 0)
    cnt = cnt_ref[...]
    for k, ik in enumerate((i1, i2)):
        hot = eidx == ik
        hot_f = jnp.where(hot, 1.0, 0.0)
        earlier = _dot(hot_f.astype(BF16), before_ref[...])
        rank = jnp.sum(jnp.where(hot, cnt + earlier, 0.0), axis=0, keepdims=True)
        rank_ref[k:k + 1, :] = rank.astype(jnp.int32)
        cnt = cnt + jnp.sum(hot_f, axis=1, keepdims=True)
    cnt_ref[...] = cnt
    count_ref[...] = jnp.broadcast_to(cnt, count_ref.shape).astype(jnp.int32)


def router(x, w_router, router_bias, tm):
    T, D = x.shape
    E = w_router.shape[1]
    s = lax.broadcasted_iota(jnp.int32, (tm, tm), 0)
    t = lax.broadcasted_iota(jnp.int32, (tm, tm), 1)
    before = jnp.where(s < t, 1.0, 0.0).astype(BF16)
    slots = pl.BlockSpec((2, tm), lambda i: (0, i))
    eid, gate, rank, counts = pl.pallas_call(
        _router_kernel,
        grid=(T // tm,),
        in_specs=[pl.BlockSpec((tm, D), lambda i: (i, 0)),
                  pl.BlockSpec((E, D), lambda i: (0, 0)),
                  pl.BlockSpec((E, 1), lambda i: (0, 0)),
                  pl.BlockSpec((tm, tm), lambda i: (0, 0))],
        out_specs=[slots, slots, slots, pl.BlockSpec((E, V7X_LANES), lambda i: (0, 0))],
        out_shape=[jax.ShapeDtypeStruct((2, T), jnp.int32), jax.ShapeDtypeStruct((2, T), F32),
                   jax.ShapeDtypeStruct((2, T), jnp.int32),
                   jax.ShapeDtypeStruct((E, V7X_LANES), jnp.int32)],
        scratch_shapes=[pltpu.VMEM((E, 1), F32)],
        compiler_params=_params("arbitrary"),
        name="router",
    )(x, w_router.T, router_bias.reshape(E, 1).astype(F32), before)
    return eid, gate, rank, counts[:, 0]


def _gelu_tanh(x):
    c = math.sqrt(2.0 / math.pi)
    return 0.5 * x * (1.0 + jnp.tanh(c * (x + 0.044715 * (x * x * x))))


ROW_COPY_UNROLL = 8


def _row_copies(pos_ref, step, tm, make):
    n_tok = pos_ref.shape[0] // 2

    def copy(r, k):
        return make(r, k, pos_ref[k * n_tok + step * tm + r])

    def start(r, c):
        copy(r, 0).start()
        copy(r, 1).start()
        return c

    def wait(r, c):
        copy(r, 0).wait()
        copy(r, 1).wait()
        return c

    lax.fori_loop(0, tm, start, 0, unroll=ROW_COPY_UNROLL)
    lax.fori_loop(0, tm, wait, 0, unroll=ROW_COPY_UNROLL)


def _dispatch_kernel(pos_ref, last_ref, x_ref, *rest, zero_fill, tile, span):
    if zero_fill:
        xs_ref, sem, zsem, zero_ref = rest
    else:
        _, xs_ref, sem = rest
    tm = x_ref.shape[0]

    if zero_fill:
        @pl.when(pl.program_id(0) == 0)
        def _():
            zero_ref[...] = jnp.zeros_like(zero_ref)

            def fill(t):
                start = pl.multiple_of(t * tile, tile)
                return pltpu.make_async_copy(zero_ref, xs_ref.at[pl.ds(start, tile), :], zsem)

            n_exp = (last_ref.shape[0] - 1) // 2
            used = last_ref[2 * n_exp]
            wanted = []
            for e in range(n_exp):
                lo, hi = last_ref[e], last_ref[n_exp + e]
                wanted += [(lo + k <= hi, lo + k) for k in range(span)]
            wanted += [(used <= t, t) for t in range(xs_ref.shape[0] // tile)]
            for op in ("start", "wait"):
                for cond, t in wanted:
                    @pl.when(cond)
                    def _():
                        getattr(fill(t), op)()

    def make(r, k, p):
        return pltpu.make_async_copy(x_ref.at[pl.ds(r, 1), :], xs_ref.at[pl.ds(p, 1), :], sem)

    _row_copies(pos_ref, pl.program_id(0), tm, make)


def _experts_kernel(te_ref, used_ref, xs_ref, wu_ref, wd_ref, ys_ref):
    del te_ref
    j = pl.program_id(0)

    @pl.when(j < used_ref[0])
    def _():
        h = _gelu_tanh(_dot(xs_ref[...].astype(BF16), wu_ref[...]))
        ys_ref[...] = _dot(h.astype(BF16), wd_ref[...])

    @pl.when(j >= used_ref[0])
    def _():
        ys_ref[...] = jnp.zeros_like(ys_ref)


def _combine_kernel(pos_ref, x_ref, gate_ref, ys_ref, g_ref, b_ref, o_ref, buf_ref, sem, *, alpha):
    tm = x_ref.shape[0]

    def make(r, k, p):
        return pltpu.make_async_copy(ys_ref.at[pl.ds(p, 1), :], buf_ref.at[k, pl.ds(r, 1), :], sem)

    _row_copies(pos_ref, pl.program_id(0), tm, make)
    gate = gate_ref[...]
    y = gate[:, 0:1] * buf_ref[0] + gate[:, 1:2] * buf_ref[1]
    o_ref[...] = _post_ln(x_ref[...], y, g_ref[...], b_ref[...], alpha)


def _moe_plan(routes, tile, n_tiles):
    counts = sum(r[3] for r in routes)
    padded = (counts + tile - 1) // tile * tile
    ends = jnp.cumsum(padded)
    tile_ends = ends // tile
    used = tile_ends[-1]
    base = ends - padded
    pos = []
    for eid, _, rank, cnt in routes:
        rows = rank + sum(jnp.where(eid == e, base[e], 0) for e in range(counts.shape[0]))
        pos.append(rows.astype(jnp.int32).reshape(-1))
        base = base + cnt
    j = jnp.arange(n_tiles, dtype=jnp.int32)
    tile_expert = jnp.sum(tile_ends[None, :] <= jnp.minimum(j, used - 1)[:, None], axis=1)
    used = used.astype(jnp.int32).reshape(1)
    hi = jnp.where(counts > 0, tile_ends - 1, -1)
    lo = (ends - padded + routes[0][3]) // tile
    last_tile = jnp.concatenate([lo.astype(jnp.int32), hi.astype(jnp.int32), used])
    return pos, tile_expert.astype(jnp.int32), used, last_tile


def _dma_params():
    return pltpu.CompilerParams(dimension_semantics=("arbitrary",), vmem_limit_bytes=VMEM_LIMIT,
                                disable_bounds_checks=True)


def moe_sparse(xs_in, routes, w_up_bf, w_down_bf, layer, g, b, alpha, tms, tile):
    D = xs_in[0].shape[1]
    _, E, _, F = w_up_bf.shape
    n_tiles = (2 * sum(x.shape[0] for x in xs_in) + tile - 1) // tile + E
    pos, tile_expert, used, last_tile = _moe_plan(routes, tile, n_tiles)
    span = (2 * sum(x.shape[0] for x in xs_in[1:]) + tile - 1) // tile + 1
    any_spec = pl.BlockSpec(memory_space=pl.ANY)
    buf_shape = jax.ShapeDtypeStruct((n_tiles * tile, D), F32)

    buf = None
    for x, p, tm in zip(xs_in, pos, tms):
        first = buf is None
        scratch = [pltpu.SemaphoreType.DMA(())]
        if first:
            scratch += [pltpu.SemaphoreType.DMA(()), pltpu.VMEM((tile, D), F32)]
        buf = pl.pallas_call(
            functools.partial(_dispatch_kernel, zero_fill=first, tile=tile, span=span),
            grid_spec=pltpu.PrefetchScalarGridSpec(
                num_scalar_prefetch=2,
                grid=(x.shape[0] // tm,),
                in_specs=[pl.BlockSpec((tm, D), lambda i, pos, last: (i, 0))] + ([] if first else [any_spec]),
                out_specs=any_spec,
                scratch_shapes=scratch,
            ),
            out_shape=buf_shape,
            input_output_aliases={} if first else {3: 0},
            compiler_params=_dma_params(),
            name="moe_dispatch",
        )(p, last_tile, x, *([] if first else [buf]))

    ys = pl.pallas_call(
        _experts_kernel,
        grid_spec=pltpu.PrefetchScalarGridSpec(
            num_scalar_prefetch=2,
            grid=(n_tiles,),
            in_specs=[pl.BlockSpec((tile, D), lambda j, te, used: (jnp.minimum(j, used[0] - 1), 0)),
                      pl.BlockSpec((None, None, D, F), lambda j, te, used: (layer, te[j], 0, 0)),
                      pl.BlockSpec((None, None, F, D), lambda j, te, used: (layer, te[j], 0, 0))],
            out_specs=pl.BlockSpec((tile, D), lambda j, te, used: (j, 0)),
        ),
        out_shape=buf_shape,
        compiler_params=_params("arbitrary"),
        name="moe_experts",
    )(tile_expert, used, buf, w_up_bf, w_down_bf)

    outs = []
    for x, p, (_, gate, _, _), tm in zip(xs_in, pos, routes, tms):
        vec = pl.BlockSpec((1, D), lambda i, pos: (0, 0))
        row = pl.BlockSpec((tm, D), lambda i, pos: (i, 0))
        outs.append(pl.pallas_call(
            functools.partial(_combine_kernel, alpha=alpha),
            grid_spec=pltpu.PrefetchScalarGridSpec(
                num_scalar_prefetch=1,
                grid=(x.shape[0] // tm,),
                in_specs=[row, pl.BlockSpec((tm, 2), lambda i, pos: (i, 0)), any_spec, vec, vec],
                out_specs=row,
                scratch_shapes=[pltpu.VMEM((2, tm, D), F32), pltpu.SemaphoreType.DMA(())],
            ),
            out_shape=jax.ShapeDtypeStruct(x.shape, F32),
            compiler_params=_dma_params(),
            name="moe_combine",
        )(p, x, gate.T, ys, g.reshape(1, D), b.reshape(1, D)))
    return outs


def _matmul_kernel(x_ref, w_ref, o_ref):
    o_ref[...] = _dot(x_ref[...].astype(BF16), w_ref[...])


def matmul(x, w_bf, tm):
    T, K = x.shape
    N = w_bf.shape[1]
    return pl.pallas_call(
        _matmul_kernel,
        grid=(T // tm,),
        in_specs=[pl.BlockSpec((tm, K), lambda i: (i, 0)), pl.BlockSpec((K, N), lambda i: (0, 0))],
        out_specs=pl.BlockSpec((tm, N), lambda i: (i, 0)),
        out_shape=jax.ShapeDtypeStruct((T, N), F32),
        compiler_params=_params("parallel"),
        name="matmul",
    )(x, w_bf)


def _pool_windows(ext_ref, tm, pos0, o_ref):
    C = ext_ref.shape[1]
    gw = C // len(POOL_WINDOWS)
    pos = pos0 + lax.broadcasted_iota(jnp.int32, (tm, 1), 0)
    for g, w in enumerate(POOL_WINDOWS):
        cols = slice(g * gw, (g + 1) * gw)
        cur = ext_ref[pl.ds(POOL_PAD, tm), cols]
        tot = cur
        for d in range(1, w):
            tot = tot + ext_ref[pl.ds(POOL_PAD - d, tm), cols]
        cnt = jnp.minimum(pos + 1, w).astype(F32)
        o_ref[:, cols] = (tot / cnt - cur).astype(o_ref.dtype)


def _pool_sample_kernel(ext_ref, o_ref, *, n_new, start_pos):
    _pool_windows(ext_ref, n_new, start_pos, o_ref)


def pool_sample(ext, n_new, start_pos):
    B, R, C = ext.shape
    assert R == POOL_PAD + n_new
    return pl.pallas_call(
        functools.partial(_pool_sample_kernel, n_new=n_new, start_pos=start_pos),
        grid=(B,),
        in_specs=[pl.BlockSpec((None, R, C), lambda b: (b, 0, 0))],
        out_specs=pl.BlockSpec((None, n_new, C), lambda b: (b, 0, 0)),
        out_shape=jax.ShapeDtypeStruct((B, n_new, C), F32),
        compiler_params=_params("parallel"),
        name="pool_sample",
    )(ext)


def _odd_out(x, d, wg_ref, sc_ref, wo_ref, g_ref, b_ref, alpha):
    ng, gw, _ = wg_ref.shape
    ys = [_dot(d[:, g * gw:(g + 1) * gw], wg_ref[g]) for g in range(ng)]
    y = (jnp.concatenate(ys, axis=1) * sc_ref[...]).astype(BF16)
    return _post_ln(x, _dot(y, wo_ref[...]), g_ref[...], b_ref[...], alpha)


def _out_odd_kernel(x_ref, d_ref, wg_ref, sc_ref, wo_ref, g_ref, b_ref, o_ref, *, alpha):
    o_ref[...] = _odd_out(x_ref[...], d_ref[...].astype(BF16), wg_ref, sc_ref, wo_ref, g_ref, b_ref, alpha)


def _odd_prompt_kernel(x_ref, wi_ref, wg_ref, sc_ref, wo_ref, g_ref, b_ref, o_ref, tail_ref, ext_ref, d_ref,
                       *, alpha):
    i = pl.program_id(0)
    tm = x_ref.shape[0]

    @pl.when(i == 0)
    def _():
        ext_ref[pl.ds(0, POOL_PAD), :] = jnp.zeros((POOL_PAD, ext_ref.shape[1]), F32)

    ext_ref[pl.ds(POOL_PAD, tm), :] = _dot(x_ref[...].astype(BF16), wi_ref[...])
    _pool_windows(ext_ref, tm, i * tm, d_ref)
    last = ext_ref[pl.ds(tm, POOL_PAD), :]
    ext_ref[pl.ds(0, POOL_PAD), :] = last
    tail_ref[...] = last
    o_ref[...] = _odd_out(x_ref[...], d_ref[...], wg_ref, sc_ref, wo_ref, g_ref, b_ref, alpha)


def odd_prompt(x, w_in_bf, w_grp_bf, scale, w_out_bf, g, b, alpha, tm):
    T, D = x.shape
    assert T % tm == 0 and tm >= POOL_PAD
    row = pl.BlockSpec((tm, D), lambda i: (i, 0))
    full = lambda a: pl.BlockSpec(a.shape, lambda i: (0,) * a.ndim)
    sc, g2, b2 = scale.reshape(1, D), g.reshape(1, D), b.reshape(1, D)
    return pl.pallas_call(
        functools.partial(_odd_prompt_kernel, alpha=alpha),
        grid=(T // tm,),
        in_specs=[row, full(w_in_bf), full(w_grp_bf), full(sc), full(w_out_bf), full(g2), full(b2)],
        out_specs=[row, pl.BlockSpec((POOL_PAD, D), lambda i: (0, 0))],
        out_shape=[jax.ShapeDtypeStruct((T, D), F32), jax.ShapeDtypeStruct((POOL_PAD, D), F32)],
        scratch_shapes=[pltpu.VMEM((POOL_PAD + tm, D), F32), pltpu.VMEM((tm, D), BF16)],
        compiler_params=_params("arbitrary"),
        name="odd_prompt",
    )(x, w_in_bf, w_grp_bf, sc, w_out_bf, g2, b2)


def out_odd(x, d, w_grp_bf, scale, w_out_bf, g, b, alpha, tm):
    T, D = x.shape
    row = pl.BlockSpec((tm, D), lambda i: (i, 0))
    full = lambda a: pl.BlockSpec(a.shape, lambda i: (0,) * a.ndim)
    sc, g2, b2 = scale.reshape(1, D), g.reshape(1, D), b.reshape(1, D)
    return pl.pallas_call(
        functools.partial(_out_odd_kernel, alpha=alpha),
        grid=(T // tm,),
        in_specs=[row, row, full(w_grp_bf), full(sc), full(w_out_bf), full(g2), full(b2)],
        out_specs=row,
        out_shape=jax.ShapeDtypeStruct((T, D), F32),
        compiler_params=_params("parallel"),
        name="out_odd",
    )(x, d, w_grp_bf, sc, w_out_bf, g2, b2)


TM_PROMPT = 512
TQ_PROMPT = 512
TK_PROMPT = 256
DECODE_PAGES = 16
DECODE_TK = 256
MOE_TILE = 512


def _ffn(xs_in, w_router, router_bias, w_up_bf, w_down_bf, layer, g, b, alpha, tms, tile):
    routes = [router(x, w_router, router_bias, tm) for x, tm in zip(xs_in, tms)]
    return moe_sparse(xs_in, routes, w_up_bf, w_down_bf, layer, g, b, alpha, tms, tile)


def kernel(x_prompt, x_sample, cache_k, cache_v, state_conv, state_pool, page_table, w_in_ab, conv_w, sb_bias, w_out_ab, w_in_c, w_grp_c, scale_c, w_out_c, w_router, router_bias, w_exp_up, w_exp_down, ln_mix_g, ln_mix_b, ln_ffn_g, ln_ffn_b):
    bp, sp_len, D = x_prompt.shape
    bs, ls, _ = x_sample.shape
    assert bp == 1
    depth = w_exp_up.shape[0]
    n_heads = sb_bias.shape[1]
    cw = conv_w.shape[2]
    aw = n_heads * HEAD_DIM
    n_pages = page_table.shape[1]
    past_len = n_pages * PAGE
    alpha = (2 * depth) ** 0.25
    Tp, Ts = bp * sp_len, bs * ls

    xp = x_prompt.reshape(Tp, D)
    xs = x_sample.reshape(Ts, D)
    outs = {n: [] for n in ("kp", "vp", "ks", "vs", "cp", "cs", "pp", "ps")}
    cache_kt = jnp.transpose(cache_k.reshape(-1, PAGE, aw), (0, 2, 1))
    cache_vt = jnp.transpose(cache_v.reshape(-1, PAGE, aw), (0, 2, 1))

    w_up_bf = w_exp_up.astype(BF16)
    w_down_bf = w_exp_down.astype(BF16)
    for layer in range(depth):
        if layer % 2 == 0:
            e = layer // 2
            w_in_bf = w_in_ab[e].astype(BF16)
            w_out_bf = w_out_ab[e].astype(BF16)
            yc, h, q, k, v, kb, vb = proj_even(xp, w_in_bf, conv_w[e], None, sp_len, TM_PROMPT, True)
            att = sb_prompt(q, kb, vb, sb_bias[e], TQ_PROMPT, TK_PROMPT)
            xp = out_even(xp, yc, att, w_out_bf, ln_mix_g[layer], ln_mix_b[layer], alpha, TM_PROMPT)
            outs["kp"].append(k.reshape(bp, sp_len, n_heads, HEAD_DIM))
            outs["vp"].append(v.reshape(bp, sp_len, n_heads, HEAD_DIM))
            outs["cp"].append(h.reshape(bp, sp_len, cw)[:, sp_len - 2:])
            st = state_conv[e]
            zero = jnp.zeros((bs, 1, cw), F32)
            p1 = jnp.concatenate([st[:, 1:2]] + [zero] * (ls - 1), axis=1).reshape(Ts, cw)
            p2 = jnp.concatenate([st[:, 0:1], st[:, 1:2]] + [zero] * (ls - 2), axis=1).reshape(Ts, cw)
            yc, h, q, k, v, kb, vb = proj_even(xs, w_in_bf, conv_w[e], (p1, p2), ls, Ts, False)
            att = sb_decode(q.reshape(bs, ls, aw), kb.reshape(bs, ls, aw), vb.reshape(bs, ls, aw),
                            cache_kt, cache_vt, page_table + e * cache_k.shape[1], sb_bias[e], n_heads,
                            DECODE_PAGES, DECODE_TK).reshape(Ts, aw)
            xs = out_even(xs, yc, att, w_out_bf, ln_mix_g[layer], ln_mix_b[layer], alpha, Ts)
            outs["ks"].append(k.reshape(bs, ls, n_heads, HEAD_DIM))
            outs["vs"].append(v.reshape(bs, ls, n_heads, HEAD_DIM))
            hs = jnp.concatenate([st, h.reshape(bs, ls, cw)], axis=1)
            outs["cs"].append(hs[:, -2:])
        else:
            o = layer // 2
            w_in_bf = w_in_c[o].astype(BF16)
            w_grp_bf = w_grp_c[o].astype(BF16)
            w_out_bf = w_out_c[o].astype(BF16)
            buf = POOL_PAD - 1
            xp, u_tail = odd_prompt(xp, w_in_bf, w_grp_bf, scale_c[o], w_out_bf,
                                    ln_mix_g[layer], ln_mix_b[layer], alpha, TM_PROMPT)
            outs["pp"].append(u_tail[None, -buf:])
            u = matmul(xs, w_in_bf, Ts)
            ext = jnp.concatenate([jnp.zeros((bs, 1, D), F32), state_pool[o], u.reshape(bs, ls, D)], axis=1)
            d = pool_sample(ext, ls, past_len).reshape(Ts, D)
            xs = out_odd(xs, d, w_grp_bf, scale_c[o], w_out_bf,
                                ln_mix_g[layer], ln_mix_b[layer], alpha, Ts)
            outs["ps"].append(ext[:, -buf:])
        xp, xs = _ffn([xp, xs], w_router, router_bias, w_up_bf, w_down_bf, layer,
                      ln_ffn_g[layer], ln_ffn_b[layer], alpha, [TM_PROMPT, Ts], MOE_TILE)

    st = lambda n: jnp.stack(outs[n])
    return (xp.reshape(bp, sp_len, D), xs.reshape(bs, ls, D),
            st("kp"), st("vp"), st("ks"), st("vs"), st("cp"), st("cs"), st("pp"), st("ps"))
```

```python
import functools
import math

import jax
import jax.numpy as jnp
from jax import lax
from jax.experimental import pallas as pl
from jax.experimental.pallas import tpu as pltpu

V7X_LANES = 128
V7X_SUBLANES = 8
V7X_VMEM_BYTES = 64 * 1024 * 1024
VMEM_LIMIT = V7X_VMEM_BYTES - 8 * 1024 * 1024

PAGE = 128
HEAD_DIM = 64
HEADS_PER_BLOCK = V7X_LANES // HEAD_DIM
POOL_WINDOWS = (2, 4, 8, 16)
POOL_PAD = 16
LN_EPS = 1e-5
N_GROUPS = 4
GROUP_SIZE = 4

BF16 = jnp.bfloat16
F32 = jnp.float32


def _params(*sem):
    return pltpu.CompilerParams(dimension_semantics=sem, vmem_limit_bytes=VMEM_LIMIT)


def _dot(a, b):
    return jnp.dot(a, b, preferred_element_type=F32)


def _dot_nt(a, b):
    return lax.dot_general(a, b, (((1,), (1,)), ((), ())), preferred_element_type=F32)


LOG2E = math.log2(math.e)


SOFTPLUS2_CLAMP = 100.0


def _softplus2(z2):
    return jnp.maximum(z2, jnp.log2(1.0 + jnp.exp2(jnp.minimum(z2, SOFTPLUS2_CLAMP))))


def _post_ln(x, sub, g, b, alpha):
    y = alpha * x + sub
    mu = jnp.mean(y, axis=-1, keepdims=True)
    yc = y - mu
    var = jnp.mean(yc * yc, axis=-1, keepdims=True)
    return yc * lax.rsqrt(var + LN_EPS) * g + b


def _store_head_expanded(o_ref, x):
    lane = lax.broadcasted_iota(jnp.int32, (1, V7X_LANES), 1)
    for h in range(x.shape[1] // HEAD_DIM):
        p, hh = divmod(h, HEADS_PER_BLOCK)
        blk = x[:, p * V7X_LANES:(p + 1) * V7X_LANES]
        o_ref[:, h * V7X_LANES:(h + 1) * V7X_LANES] = jnp.where(
            (lane // HEAD_DIM) == hh, blk, 0.0).astype(o_ref.dtype)


def _proj_even_kernel(*refs, cw, seq_len, has_state, expand_heads):
    if has_state:
        (x_ref, w_ref, cwt_ref, p1_ref, p2_ref,
         yc_ref, h_ref, q_ref, k_ref, v_ref, kb_ref, vb_ref, carry_ref) = refs
    else:
        (x_ref, w_ref, cwt_ref,
         yc_ref, h_ref, q_ref, k_ref, v_ref, kb_ref, vb_ref, carry_ref) = refs
    i = pl.program_id(0)
    tm = x_ref.shape[0]
    xb = x_ref[...].astype(BF16)

    def sec(n):
        return _dot(xb, w_ref[:, n * cw:(n + 1) * cw])

    gb, gc, xv = sec(0), sec(1), sec(2)
    h = gc * xv
    hm1 = pltpu.roll(h, 1, axis=0)
    hm2 = pltpu.roll(h, 2, axis=0)
    row = lax.broadcasted_iota(jnp.int32, (tm, 1), 0)
    if has_state:
        l = row % seq_len
        hm1 = jnp.where(l < 1, p1_ref[...], hm1)
        hm2 = jnp.where(l < 2, p2_ref[...], hm2)
    else:
        @pl.when(i == 0)
        def _():
            carry_ref[...] = jnp.zeros_like(carry_ref)
        prev = carry_ref[...]
        last = V7X_SUBLANES - 1
        hm1 = jnp.where(row == 0, prev[last:last + 1, :], hm1)
        hm2 = jnp.where(row == 0, prev[last - 1:last, :], hm2)
        hm2 = jnp.where(row == 1, prev[last:last + 1, :], hm2)
        carry_ref[...] = h[tm - V7X_SUBLANES:, :]
    cwt = cwt_ref[...]
    conv = cwt[0:1, :] * hm2 + cwt[1:2, :] * hm1 + cwt[2:3, :] * h
    yc_ref[...] = (gb * conv).astype(BF16)
    h_ref[...] = h
    q_ref[...] = (sec(3) * (HEAD_DIM ** -0.5 * LOG2E)).astype(BF16)
    k = sec(4)
    v = sec(5)
    k_ref[...] = k
    v_ref[...] = v
    if expand_heads:
        _store_head_expanded(kb_ref, k)
        _store_head_expanded(vb_ref, v)
    else:
        kb_ref[...] = k.astype(BF16)
        vb_ref[...] = v.astype(BF16)


def proj_even(x, w_bf, conv_w, state_rows, seq_len, tm, expand_heads):
    T, D = x.shape
    cw = conv_w.shape[1]
    assert w_bf.shape == (D, 6 * cw) and T % tm == 0
    has_state = state_rows is not None
    if has_state:
        assert T == tm
    bw = cw * HEADS_PER_BLOCK if expand_heads else cw
    row_in = pl.BlockSpec((tm, D), lambda i: (i, 0))
    row_out = pl.BlockSpec((tm, cw), lambda i: (i, 0))
    row_kv = pl.BlockSpec((tm, bw), lambda i: (i, 0))
    in_specs = [row_in,
                pl.BlockSpec((D, 6 * cw), lambda i: (0, 0)),
                pl.BlockSpec((3, cw), lambda i: (0, 0))]
    args = [x, w_bf, conv_w]
    if has_state:
        in_specs += [row_out, row_out]
        args += list(state_rows)
    out_shape = [jax.ShapeDtypeStruct((T, cw), d) for d in (BF16, F32, BF16, F32, F32)]
    out_shape += [jax.ShapeDtypeStruct((T, bw), BF16)] * 2
    return pl.pallas_call(
        functools.partial(_proj_even_kernel, cw=cw, seq_len=seq_len, has_state=has_state,
                          expand_heads=expand_heads),
        grid=(T // tm,),
        in_specs=in_specs,
        out_specs=[row_out] * 5 + [row_kv] * 2,
        out_shape=out_shape,
        scratch_shapes=[pltpu.VMEM((V7X_SUBLANES, cw), F32)],
        compiler_params=_params("arbitrary"),
        name="proj_even",
    )(*args)


def _sb_weights(z, nu, rs, mask):
    tk = nu.shape[0]
    groups = z.shape[1] // tk
    cols = lambda x, g: x[:, g * tk:(g + 1) * tk]
    sp = _softplus2(z)
    if mask is not None:
        mask = jnp.concatenate([mask] * groups, axis=1)
        spm = jnp.where(mask, sp, 0.0)
    else:
        spm = sp
    hi = spm.astype(BF16)
    off = [_dot(cols(hi, g), nu) + rs[g] for g in range(groups)]
    a = jnp.exp2((z - sp) + jnp.concatenate(off, axis=1))
    if mask is not None:
        a = jnp.where(mask, a, 0.0)
    return a.astype(BF16), [rs[g] - jnp.sum(cols(spm, g), axis=1, keepdims=True) for g in range(groups)]


def _sb_prompt_kernel(bias_ref, q_ref, k_ref, v_ref, nu_ref, o_ref, acc_ref, r_ref,
                      z0_ref, z1_ref, a0_ref, a1_ref, *, tk):
    p = pl.program_id(0)
    i = pl.program_id(1)
    tq = q_ref.shape[0]
    nd = tq // tk
    groups = HEADS_PER_BLOCK
    col = lax.broadcasted_iota(jnp.int32, (1, groups * tk), 1)
    bias = jnp.zeros((1, groups * tk), F32)
    for g in range(groups):
        bias = jnp.where(col // tk == g, bias_ref[groups * p + g] * LOG2E, bias)
    acc_ref[...] = jnp.zeros_like(acc_ref)
    r_ref[...] = jnp.zeros_like(r_ref)

    def rows(ref, chunk):
        start = pl.multiple_of(chunk * tk, tk)
        return jnp.concatenate([ref[pl.ds(start, tk), g * V7X_LANES:(g + 1) * V7X_LANES]
                                for g in range(groups)], axis=0)

    def logits(chunk):
        return _dot_nt(q_ref[...], rows(k_ref, chunk)) + bias

    def weights(z, mask):
        a, rs = _sb_weights(z, nu_ref[...], [r_ref[g] for g in range(groups)], mask)
        for g in range(groups):
            r_ref[g] = rs[g]
        return a

    qpos = i * tq + lax.broadcasted_iota(jnp.int32, (tq, 1), 0)
    for d in reversed(range(nd)):
        chunk = i * nd + d
        kpos = chunk * tk + lax.broadcasted_iota(jnp.int32, (1, tk), 1)
        acc_ref[...] += _dot(weights(logits(chunk), kpos < qpos), rows(v_ref, chunk))

    n = i * nd
    z_refs, a_refs = (z0_ref, z1_ref), (a0_ref, a1_ref)
    z0_ref[...] = logits(jnp.maximum(n - 1, 0))
    a1_ref[...] = jnp.zeros_like(a1_ref)

    def stage(cur, chunk):
        nxt = 1 - cur
        z_refs[nxt][...] = logits(jnp.maximum(chunk - 1, 0))
        acc_ref[...] += _dot(a_refs[nxt][...], rows(v_ref, chunk + 1))
        a_refs[cur][...] = weights(z_refs[cur][...], None)

    def body(t, carry):
        for s in range(2):
            stage(s, n - 1 - 2 * t - s)
        return carry

    assert nd % 2 == 0
    lax.fori_loop(0, n // 2, body, 0)
    acc_ref[...] += _dot(a1_ref[...], rows(v_ref, 0))
    o_ref[...] = acc_ref[...].astype(o_ref.dtype)


def _neg_upper(n):
    j = lax.broadcasted_iota(jnp.int32, (n, n), 0)
    s = lax.broadcasted_iota(jnp.int32, (n, n), 1)
    return jnp.where(j > s, -1.0, 0.0).astype(BF16)


def sb_prompt(q_bf, kx_bf, vx_bf, bias, tq, tk):
    T, W = q_bf.shape
    assert T % tq == 0 and tq % tk == 0 and W % V7X_LANES == 0
    assert kx_bf.shape == (T, HEADS_PER_BLOCK * W)
    nblk = W // V7X_LANES
    qspec = pl.BlockSpec((tq, V7X_LANES), lambda p, i: (i, p))
    kvspec = pl.BlockSpec((T, HEADS_PER_BLOCK * V7X_LANES), lambda p, i: (0, p))
    return pl.pallas_call(
        functools.partial(_sb_prompt_kernel, tk=tk),
        grid=(nblk, T // tq),
        in_specs=[pl.BlockSpec(memory_space=pltpu.SMEM), qspec, kvspec, kvspec,
                  pl.BlockSpec((tk, tk), lambda p, i: (0, 0))],
        out_specs=qspec,
        out_shape=jax.ShapeDtypeStruct((T, W), BF16),
        scratch_shapes=[pltpu.VMEM((tq, V7X_LANES), F32),
                        pltpu.VMEM((HEADS_PER_BLOCK, tq, 1), F32),
                        pltpu.VMEM((tq, HEADS_PER_BLOCK * tk), F32),
                        pltpu.VMEM((tq, HEADS_PER_BLOCK * tk), F32),
                        pltpu.VMEM((tq, HEADS_PER_BLOCK * tk), BF16),
                        pltpu.VMEM((tq, HEADS_PER_BLOCK * tk), BF16)],
        compiler_params=_params("arbitrary", "arbitrary"),
        name="sb_prompt",
    )(bias, q_bf, kx_bf, vx_bf, _neg_upper(tk))


def _sb_decode_kernel(pt_ref, qbd_ref, brow_ref, knt_ref, vnt_ref, nu_ref, *rest, pages, n_new, n_heads):
    kt_refs = rest[:pages]
    vt_refs = rest[pages:2 * pages]
    o_ref, acc_ref, r_ref = rest[2 * pages:]
    j = pl.program_id(1)
    qbd = qbd_ref[...]
    brow = brow_ref[...]
    nu = nu_ref[...]
    tk = nu.shape[0]
    rows = qbd.shape[0]

    @pl.when(j == 0)
    def _():
        l = lax.broadcasted_iota(jnp.int32, (rows, 1), 0) // n_heads
        c = lax.broadcasted_iota(jnp.int32, (1, PAGE), 1)
        z = _dot(qbd, knt_ref[...]) + brow
        a, (r,) = _sb_weights(z, nu[:PAGE, :PAGE], [jnp.zeros((rows, 1), F32)], c < l)
        acc_ref[...] = _dot_nt(a, vnt_ref[...])
        r_ref[...] = r

    kt = jnp.concatenate([ref[...].astype(BF16) for ref in kt_refs], axis=1)
    z = _dot(qbd, kt) + brow
    r = r_ref[...]
    parts = []
    for c in reversed(range(pages * PAGE // tk)):
        a, (r,) = _sb_weights(z[:, c * tk:(c + 1) * tk], nu, [r], None)
        parts.append(a)
    vt = jnp.concatenate([ref[...].astype(BF16) for ref in vt_refs], axis=1)
    acc_ref[...] += _dot_nt(jnp.concatenate(parts[::-1], axis=1), vt)
    r_ref[...] = r

    @pl.when(j == pl.num_programs(1) - 1)
    def _():
        acc = acc_ref[...]
        rr = lax.broadcasted_iota(jnp.int32, acc.shape, 0) % n_heads
        cc = lax.broadcasted_iota(jnp.int32, acc.shape, 1) // HEAD_DIM
        accm = jnp.where(rr == cc, acc, 0.0)
        o_ref[...] = jnp.sum(accm.reshape(n_new, n_heads, acc.shape[1]), axis=1).astype(o_ref.dtype)


def sb_decode(q_bf, k_new_bf, v_new_bf, cache_kt, cache_vt, page_table, bias, n_heads, pages, tk):
    B, L, W = q_bf.shape
    n_pages = page_table.shape[1]
    assert n_pages % pages == 0 and W == n_heads * HEAD_DIM and L <= PAGE
    assert (pages * PAGE) % tk == 0 and tk % PAGE == 0
    rows = L * n_heads
    col_head = jnp.arange(W, dtype=jnp.int32) // HEAD_DIM
    keep = (jnp.arange(rows, dtype=jnp.int32) % n_heads)[:, None] == col_head[None, :]
    qbd = jnp.where(keep[None], jnp.repeat(q_bf, n_heads, axis=1), jnp.zeros((), BF16))
    brow = jnp.tile(bias.astype(F32) * LOG2E, L).reshape(rows, 1)
    pad = ((0, 0), (0, PAGE - L), (0, 0))
    knt = jnp.transpose(jnp.pad(k_new_bf, pad), (0, 2, 1))
    vnt = jnp.transpose(jnp.pad(v_new_bf, pad), (0, 2, 1))
    steps = n_pages // pages

    def page_map(t):
        def index(b, j, pt):
            return (pt[b * n_pages + (steps - 1 - j) * pages + t], 0, 0)
        return index

    per_b = lambda shape: pl.BlockSpec((None,) + shape, lambda b, j, pt: (b, 0, 0))
    page_specs = [pl.BlockSpec((None, W, PAGE), page_map(t)) for t in range(pages)]
    grid_spec = pltpu.PrefetchScalarGridSpec(
        num_scalar_prefetch=1,
        grid=(B, steps),
        in_specs=[per_b((rows, W)),
                  pl.BlockSpec((rows, 1), lambda b, j, pt: (0, 0)),
                  per_b((W, PAGE)), per_b((W, PAGE)),
                  pl.BlockSpec((tk, tk), lambda b, j, pt: (0, 0))] + page_specs + page_specs,
        out_specs=per_b((L, W)),
        scratch_shapes=[pltpu.VMEM((rows, W), F32), pltpu.VMEM((rows, 1), F32)],
    )
    return pl.pallas_call(
        functools.partial(_sb_decode_kernel, pages=pages, n_new=L, n_heads=n_heads),
        grid_spec=grid_spec,
        out_shape=jax.ShapeDtypeStruct((B, L, W), F32),
        compiler_params=_params("arbitrary", "arbitrary"),
        name="sb_decode",
    )(page_table.reshape(-1), qbd, brow, knt, vnt, _neg_upper(tk),
      *([cache_kt] * pages), *([cache_vt] * pages))


def _out_even_kernel(x_ref, yc_ref, att_ref, wt_ref, wb_ref, g_ref, b_ref, o_ref, *, alpha):
    m = _dot(yc_ref[...], wt_ref[...]) + _dot(att_ref[...].astype(BF16), wb_ref[...])
    o_ref[...] = _post_ln(x_ref[...], m, g_ref[...], b_ref[...], alpha)


def out_even(x, yc, att, w_out_bf, g, b, alpha, tm):
    T, D = x.shape
    cw = yc.shape[1]
    row = lambda w: pl.BlockSpec((tm, w), lambda i: (i, 0))
    full = lambda a: pl.BlockSpec(a.shape, lambda i: (0,) * a.ndim)
    wt, wb = w_out_bf[:cw], w_out_bf[cw:]
    g2, b2 = g.reshape(1, D), b.reshape(1, D)
    return pl.pallas_call(
        functools.partial(_out_even_kernel, alpha=alpha),
        grid=(T // tm,),
        in_specs=[row(D), row(cw), row(att.shape[1]), full(wt), full(wb), full(g2), full(b2)],
        out_specs=row(D),
        out_shape=jax.ShapeDtypeStruct((T, D), F32),
        compiler_params=_params("parallel"),
        name="out_even",
    )(x, yc, att, wt, wb, g2, b2)


def _router_kernel(x_ref, wrt_ref, rb_ref, before_ref, eid_ref, gate_ref, rank_ref, count_ref, cnt_ref):
    n_exp = wrt_ref.shape[0]
    logits = lax.dot_general(wrt_ref[...], x_ref[...], (((1,), (1,)), ((), ())),
                             precision=lax.Precision.HIGHEST, preferred_element_type=F32)
    s = jax.nn.sigmoid(logits)
    sel = s + rb_ref[...]
    srow = [s[e:e + 1, :] for e in range(n_exp)]
    row = [sel[e:e + 1, :] for e in range(n_exp)]
    best_score, best_g = None, None
    for g in range(N_GROUPS):
        m = row[g * GROUP_SIZE:(g + 1) * GROUP_SIZE]
        score = None
        for a in range(GROUP_SIZE):
            for b in range(a + 1, GROUP_SIZE):
                pair = m[a] + m[b]
                score = pair if score is None else jnp.maximum(score, pair)
        if best_score is None:
            best_score, best_g = score, jnp.zeros_like(score, dtype=jnp.int32)
        else:
            upd = score > best_score
            best_score = jnp.where(upd, score, best_score)
            best_g = jnp.where(upd, g, best_g)
    neg = -jnp.inf
    masked = [jnp.where(best_g == (e // GROUP_SIZE), row[e], neg) for e in range(n_exp)]

    def first_argmax(vals):
        bv, bi = vals[0], jnp.zeros_like(best_g)
        for e in range(1, n_exp):
            upd = vals[e] > bv
            bv = jnp.where(upd, vals[e], bv)
            bi = jnp.where(upd, e, bi)
        return bi

    i1 = first_argmax(masked)
    i2 = first_argmax([jnp.where(i1 == e, neg, masked[e]) for e in range(n_exp)])
    g1 = sum(jnp.where(i1 == e, srow[e], 0.0) for e in range(n_exp))
    g2 = sum(jnp.where(i2 == e, srow[e], 0.0) for e in range(n_exp))
    tot = g1 + g2
    eid_ref[0:1, :] = i1
    eid_ref[1:2, :] = i2
    gate_ref[0:1, :] = g1 / tot
    gate_ref[1:2, :] = g2 / tot

    @pl.when(pl.program_id(0) == 0)
    def _():
        cnt_ref[...] = jnp.zeros_like(cnt_ref)

    tm = x_ref.shape[0]
    eidx = lax.broadcasted_iota(jnp.int32, (n_exp, tm), 0)
    cnt = cnt_ref[...]
    for k, ik in enumerate((i1, i2)):
        hot = eidx == ik
        hot_f = jnp.where(hot, 1.0, 0.0)
        earlier = _dot(hot_f.astype(BF16), before_ref[...])
        rank = jnp.sum(jnp.where(hot, cnt + earlier, 0.0), axis=0, keepdims=True)
        rank_ref[k:k + 1, :] = rank.astype(jnp.int32)
        cnt = cnt + jnp.sum(hot_f, axis=1, keepdims=True)
    cnt_ref[...] = cnt
    count_ref[...] = jnp.broadcast_to(cnt, count_ref.shape).astype(jnp.int32)


def router(x, w_router, router_bias, tm):
    T, D = x.shape
    E = w_router.shape[1]
    s = lax.broadcasted_iota(jnp.int32, (tm, tm), 0)
    t = lax.broadcasted_iota(jnp.int32, (tm, tm), 1)
    before = jnp.where(s < t, 1.0, 0.0).astype(BF16)
    slots = pl.BlockSpec((2, tm), lambda i: (0, i))
    eid, gate, rank, counts = pl.pallas_call(
        _router_kernel,
        grid=(T // tm,),
        in_specs=[pl.BlockSpec((tm, D), lambda i: (i, 0)),
                  pl.BlockSpec((E, D), lambda i: (0, 0)),
                  pl.BlockSpec((E, 1), lambda i: (0, 0)),
                  pl.BlockSpec((tm, tm), lambda i: (0, 0))],
        out_specs=[slots, slots, slots, pl.BlockSpec((E, V7X_LANES), lambda i: (0, 0))],
        out_shape=[jax.ShapeDtypeStruct((2, T), jnp.int32), jax.ShapeDtypeStruct((2, T), F32),
                   jax.ShapeDtypeStruct((2, T), jnp.int32),
                   jax.ShapeDtypeStruct((E, V7X_LANES), jnp.int32)],
        scratch_shapes=[pltpu.VMEM((E, 1), F32)],
        compiler_params=_params("arbitrary"),
        name="router",
    )(x, w_router.T, router_bias.reshape(E, 1).astype(F32), before)
    return eid, gate, rank, counts[:, 0]


def _gelu_tanh(x):
    c = math.sqrt(2.0 / math.pi)
    return 0.5 * x * (1.0 + jnp.tanh(c * (x + 0.044715 * (x * x * x))))


ROW_COPY_UNROLL = 8


def _row_copies(pos_ref, step, tm, make):
    n_tok = pos_ref.shape[0] // 2

    def copy(r, k):
        return make(r, k, pos_ref[k * n_tok + step * tm + r])

    def start(r, c):
        copy(r, 0).start(priority=0)
        copy(r, 1).start(priority=1)
        return c

    def wait(r, c):
        copy(r, 0).wait()
        copy(r, 1).wait()
        return c

    lax.fori_loop(0, tm, start, 0, unroll=ROW_COPY_UNROLL)
    lax.fori_loop(0, tm, wait, 0, unroll=ROW_COPY_UNROLL)


def _dispatch_kernel(pos_ref, last_ref, x_ref, *rest, zero_fill, tile, span):
    if zero_fill:
        xs_ref, sem, zsem, zero_ref = rest
    else:
        _, xs_ref, sem = rest
    tm = x_ref.shape[0]

    if zero_fill:
        @pl.when(pl.program_id(0) == 0)
        def _():
            zero_ref[...] = jnp.zeros_like(zero_ref)

            def fill(t):
                start = pl.multiple_of(t * tile, tile)
                return pltpu.make_async_copy(zero_ref, xs_ref.at[pl.ds(start, tile), :], zsem)

            n_exp = (last_ref.shape[0] - 1) // 2
            used = last_ref[2 * n_exp]
            wanted = []
            for e in range(n_exp):
                lo, hi = last_ref[e], last_ref[n_exp + e]
                wanted += [(lo + k <= hi, lo + k) for k in range(span)]
            wanted += [(used <= t, t) for t in range(xs_ref.shape[0] // tile)]
            for op in ("start", "wait"):
                for cond, t in wanted:
                    @pl.when(cond)
                    def _():
                        getattr(fill(t), op)()

    def make(r, k, p):
        return pltpu.make_async_copy(x_ref.at[pl.ds(r, 1), :], xs_ref.at[pl.ds(p, 1), :], sem)

    _row_copies(pos_ref, pl.program_id(0), tm, make)


def _experts_kernel(te_ref, used_ref, xs_ref, wu_ref, wd_ref, ys_ref):
    del te_ref
    j = pl.program_id(0)

    @pl.when(j < used_ref[0])
    def _():
        h = _gelu_tanh(_dot(xs_ref[...].astype(BF16), wu_ref[...]))
        ys_ref[...] = _dot(h.astype(BF16), wd_ref[...])

    @pl.when(j >= used_ref[0])
    def _():
        ys_ref[...] = jnp.zeros_like(ys_ref)


def _combine_kernel(pos_ref, x_ref, gate_ref, ys_ref, g_ref, b_ref, o_ref, buf_ref, sem, *, alpha):
    tm = x_ref.shape[0]

    def make(r, k, p):
        return pltpu.make_async_copy(ys_ref.at[pl.ds(p, 1), :], buf_ref.at[k, pl.ds(r, 1), :], sem)

    _row_copies(pos_ref, pl.program_id(0), tm, make)
    gate = gate_ref[...]
    y = gate[:, 0:1] * buf_ref[0] + gate[:, 1:2] * buf_ref[1]
    o_ref[...] = _post_ln(x_ref[...], y, g_ref[...], b_ref[...], alpha)


def _moe_plan(routes, tile, n_tiles):
    counts = sum(r[3] for r in routes)
    padded = (counts + tile - 1) // tile * tile
    ends = jnp.cumsum(padded)
    tile_ends = ends // tile
    used = tile_ends[-1]
    base = ends - padded
    pos = []
    for eid, _, rank, cnt in routes:
        rows = rank + sum(jnp.where(eid == e, base[e], 0) for e in range(counts.shape[0]))
        pos.append(rows.astype(jnp.int32).reshape(-1))
        base = base + cnt
    j = jnp.arange(n_tiles, dtype=jnp.int32)
    tile_expert = jnp.sum(tile_ends[None, :] <= jnp.minimum(j, used - 1)[:, None], axis=1)
    used = used.astype(jnp.int32).reshape(1)
    hi = jnp.where(counts > 0, tile_ends - 1, -1)
    lo = (ends - padded + routes[0][3]) // tile
    last_tile = jnp.concatenate([lo.astype(jnp.int32), hi.astype(jnp.int32), used])
    return pos, tile_expert.astype(jnp.int32), used, last_tile


def _dma_params():
    return pltpu.CompilerParams(dimension_semantics=("arbitrary",), vmem_limit_bytes=VMEM_LIMIT,
                                disable_bounds_checks=True)


def moe_sparse(xs_in, routes, w_up_bf, w_down_bf, layer, g, b, alpha, tms, tile):
    D = xs_in[0].shape[1]
    _, E, _, F = w_up_bf.shape
    n_tiles = (2 * sum(x.shape[0] for x in xs_in) + tile - 1) // tile + E
    pos, tile_expert, used, last_tile = _moe_plan(routes, tile, n_tiles)
    span = (2 * sum(x.shape[0] for x in xs_in[1:]) + tile - 1) // tile + 1
    any_spec = pl.BlockSpec(memory_space=pl.ANY)
    buf_shape = jax.ShapeDtypeStruct((n_tiles * tile, D), F32)

    buf = None
    for x, p, tm in zip(xs_in, pos, tms):
        first = buf is None
        scratch = [pltpu.SemaphoreType.DMA(())]
        if first:
            scratch += [pltpu.SemaphoreType.DMA(()), pltpu.VMEM((tile, D), F32)]
        buf = pl.pallas_call(
            functools.partial(_dispatch_kernel, zero_fill=first, tile=tile, span=span),
            grid_spec=pltpu.PrefetchScalarGridSpec(
                num_scalar_prefetch=2,
                grid=(x.shape[0] // tm,),
                in_specs=[pl.BlockSpec((tm, D), lambda i, pos, last: (i, 0))] + ([] if first else [any_spec]),
                out_specs=any_spec,
                scratch_shapes=scratch,
            ),
            out_shape=buf_shape,
            input_output_aliases={} if first else {3: 0},
            compiler_params=_dma_params(),
            name="moe_dispatch",
        )(p, last_tile, x, *([] if first else [buf]))

    ys = pl.pallas_call(
        _experts_kernel,
        grid_spec=pltpu.PrefetchScalarGridSpec(
            num_scalar_prefetch=2,
            grid=(n_tiles,),
            in_specs=[pl.BlockSpec((tile, D), lambda j, te, used: (jnp.minimum(j, used[0] - 1), 0)),
                      pl.BlockSpec((None, None, D, F), lambda j, te, used: (layer, te[j], 0, 0)),
                      pl.BlockSpec((None, None, F, D), lambda j, te, used: (layer, te[j], 0, 0))],
            out_specs=pl.BlockSpec((tile, D), lambda j, te, used: (j, 0)),
        ),
        out_shape=buf_shape,
        compiler_params=_params("arbitrary"),
        name="moe_experts",
    )(tile_expert, used, buf, w_up_bf, w_down_bf)

    outs = []
    for x, p, (_, gate, _, _), tm in zip(xs_in, pos, routes, tms):
        vec = pl.BlockSpec((1, D), lambda i, pos: (0, 0))
        row = pl.BlockSpec((tm, D), lambda i, pos: (i, 0))
        outs.append(pl.pallas_call(
            functools.partial(_combine_kernel, alpha=alpha),
            grid_spec=pltpu.PrefetchScalarGridSpec(
                num_scalar_prefetch=1,
                grid=(x.shape[0] // tm,),
                in_specs=[row, pl.BlockSpec((tm, 2), lambda i, pos: (i, 0)), any_spec, vec, vec],
                out_specs=row,
                scratch_shapes=[pltpu.VMEM((2, tm, D), F32), pltpu.SemaphoreType.DMA(())],
            ),
            out_shape=jax.ShapeDtypeStruct(x.shape, F32),
            compiler_params=_dma_params(),
            name="moe_combine",
        )(p, x, gate.T, ys, g.reshape(1, D), b.reshape(1, D)))
    return outs


def _matmul_kernel(x_ref, w_ref, o_ref):
    o_ref[...] = _dot(x_ref[...].astype(BF16), w_ref[...])


def matmul(x, w_bf, tm):
    T, K = x.shape
    N = w_bf.shape[1]
    return pl.pallas_call(
        _matmul_kernel,
        grid=(T // tm,),
        in_specs=[pl.BlockSpec((tm, K), lambda i: (i, 0)), pl.BlockSpec((K, N), lambda i: (0, 0))],
        out_specs=pl.BlockSpec((tm, N), lambda i: (i, 0)),
        out_shape=jax.ShapeDtypeStruct((T, N), F32),
        compiler_params=_params("parallel"),
        name="matmul",
    )(x, w_bf)


def _pool_windows(ext_ref, tm, pos0, o_ref):
    C = ext_ref.shape[1]
    gw = C // len(POOL_WINDOWS)
    pos = pos0 + lax.broadcasted_iota(jnp.int32, (tm, 1), 0)
    for g, w in enumerate(POOL_WINDOWS):
        cols = slice(g * gw, (g + 1) * gw)
        cur = ext_ref[pl.ds(POOL_PAD, tm), cols]
        tot = cur
        for d in range(1, w):
            tot = tot + ext_ref[pl.ds(POOL_PAD - d, tm), cols]
        cnt = jnp.minimum(pos + 1, w).astype(F32)
        o_ref[:, cols] = (tot / cnt - cur).astype(o_ref.dtype)


def _pool_sample_kernel(ext_ref, o_ref, *, n_new, start_pos):
    _pool_windows(ext_ref, n_new, start_pos, o_ref)


def pool_sample(ext, n_new, start_pos):
    B, R, C = ext.shape
    assert R == POOL_PAD + n_new
    return pl.pallas_call(
        functools.partial(_pool_sample_kernel, n_new=n_new, start_pos=start_pos),
        grid=(B,),
        in_specs=[pl.BlockSpec((None, R, C), lambda b: (b, 0, 0))],
        out_specs=pl.BlockSpec((None, n_new, C), lambda b: (b, 0, 0)),
        out_shape=jax.ShapeDtypeStruct((B, n_new, C), F32),
        compiler_params=_params("parallel"),
        name="pool_sample",
    )(ext)


def _odd_out(x, d, wg_ref, sc_ref, wo_ref, g_ref, b_ref, alpha):
    ng, gw, _ = wg_ref.shape
    ys = [_dot(d[:, g * gw:(g + 1) * gw], wg_ref[g]) for g in range(ng)]
    y = (jnp.concatenate(ys, axis=1) * sc_ref[...]).astype(BF16)
    return _post_ln(x, _dot(y, wo_ref[...]), g_ref[...], b_ref[...], alpha)


def _out_odd_kernel(x_ref, d_ref, wg_ref, sc_ref, wo_ref, g_ref, b_ref, o_ref, *, alpha):
    o_ref[...] = _odd_out(x_ref[...], d_ref[...].astype(BF16), wg_ref, sc_ref, wo_ref, g_ref, b_ref, alpha)


def _odd_prompt_kernel(x_ref, wi_ref, wg_ref, sc_ref, wo_ref, g_ref, b_ref, o_ref, tail_ref, ext_ref, d_ref,
                       *, alpha):
    i = pl.program_id(0)
    tm = x_ref.shape[0]

    @pl.when(i == 0)
    def _():
        ext_ref[pl.ds(0, POOL_PAD), :] = jnp.zeros((POOL_PAD, ext_ref.shape[1]), F32)

    ext_ref[pl.ds(POOL_PAD, tm), :] = _dot(x_ref[...].astype(BF16), wi_ref[...])
    _pool_windows(ext_ref, tm, i * tm, d_ref)
    last = ext_ref[pl.ds(tm, POOL_PAD), :]
    ext_ref[pl.ds(0, POOL_PAD), :] = last
    tail_ref[...] = last
    o_ref[...] = _odd_out(x_ref[...], d_ref[...], wg_ref, sc_ref, wo_ref, g_ref, b_ref, alpha)


def odd_prompt(x, w_in_bf, w_grp_bf, scale, w_out_bf, g, b, alpha, tm):
    T, D = x.shape
    assert T % tm == 0 and tm >= POOL_PAD
    row = pl.BlockSpec((tm, D), lambda i: (i, 0))
    full = lambda a: pl.BlockSpec(a.shape, lambda i: (0,) * a.ndim)
    sc, g2, b2 = scale.reshape(1, D), g.reshape(1, D), b.reshape(1, D)
    return pl.pallas_call(
        functools.partial(_odd_prompt_kernel, alpha=alpha),
        grid=(T // tm,),
        in_specs=[row, full(w_in_bf), full(w_grp_bf), full(sc), full(w_out_bf), full(g2), full(b2)],
        out_specs=[row, pl.BlockSpec((POOL_PAD, D), lambda i: (0, 0))],
        out_shape=[jax.ShapeDtypeStruct((T, D), F32), jax.ShapeDtypeStruct((POOL_PAD, D), F32)],
        scratch_shapes=[pltpu.VMEM((POOL_PAD + tm, D), F32), pltpu.VMEM((tm, D), BF16)],
        compiler_params=_params("arbitrary"),
        name="odd_prompt",
    )(x, w_in_bf, w_grp_bf, sc, w_out_bf, g2, b2)


def out_odd(x, d, w_grp_bf, scale, w_out_bf, g, b, alpha, tm):
    T, D = x.shape
    row = pl.BlockSpec((tm, D), lambda i: (i, 0))
    full = lambda a: pl.BlockSpec(a.shape, lambda i: (0,) * a.ndim)
    sc, g2, b2 = scale.reshape(1, D), g.reshape(1, D), b.reshape(1, D)
    return pl.pallas_call(
        functools.partial(_out_odd_kernel, alpha=alpha),
        grid=(T // tm,),
        in_specs=[row, row, full(w_grp_bf), full(sc), full(w_out_bf), full(g2), full(b2)],
        out_specs=row,
        out_shape=jax.ShapeDtypeStruct((T, D), F32),
        compiler_params=_params("parallel"),
        name="out_odd",
    )(x, d, w_grp_bf, sc, w_out_bf, g2, b2)


TM_PROMPT = 512
TQ_PROMPT = 512
TK_PROMPT = 256
DECODE_PAGES = 16
DECODE_TK = 256
MOE_TILE = 512


def _ffn(xs_in, w_router, router_bias, w_up_bf, w_down_bf, layer, g, b, alpha, tms, tile):
    routes = [router(x, w_router, router_bias, tm) for x, tm in zip(xs_in, tms)]
    return moe_sparse(xs_in, routes, w_up_bf, w_down_bf, layer, g, b, alpha, tms, tile)


def kernel(x_prompt, x_sample, cache_k, cache_v, state_conv, state_pool, page_table, w_in_ab, conv_w, sb_bias, w_out_ab, w_in_c, w_grp_c, scale_c, w_out_c, w_router, router_bias, w_exp_up, w_exp_down, ln_mix_g, ln_mix_b, ln_ffn_g, ln_ffn_b):
    bp, sp_len, D = x_prompt.shape
    bs, ls, _ = x_sample.shape
    assert bp == 1
    depth = w_exp_up.shape[0]
    n_heads = sb_bias.shape[1]
    cw = conv_w.shape[2]
    aw = n_heads * HEAD_DIM
    n_pages = page_table.shape[1]
    past_len = n_pages * PAGE
    alpha = (2 * depth) ** 0.25
    Tp, Ts = bp * sp_len, bs * ls

    xp = x_prompt.reshape(Tp, D)
    xs = x_sample.reshape(Ts, D)
    outs = {n: [] for n in ("kp", "vp", "ks", "vs", "cp", "cs", "pp", "ps")}
    cache_kt = jnp.transpose(cache_k.reshape(-1, PAGE, aw), (0, 2, 1))
    cache_vt = jnp.transpose(cache_v.reshape(-1, PAGE, aw), (0, 2, 1))

    w_up_bf = w_exp_up.astype(BF16)
    w_down_bf = w_exp_down.astype(BF16)
    for layer in range(depth):
        if layer % 2 == 0:
            e = layer // 2
            w_in_bf = w_in_ab[e].astype(BF16)
            w_out_bf = w_out_ab[e].astype(BF16)
            yc, h, q, k, v, kb, vb = proj_even(xp, w_in_bf, conv_w[e], None, sp_len, TM_PROMPT, True)
            att = sb_prompt(q, kb, vb, sb_bias[e], TQ_PROMPT, TK_PROMPT)
            xp = out_even(xp, yc, att, w_out_bf, ln_mix_g[layer], ln_mix_b[layer], alpha, TM_PROMPT)
            outs["kp"].append(k.reshape(bp, sp_len, n_heads, HEAD_DIM))
            outs["vp"].append(v.reshape(bp, sp_len, n_heads, HEAD_DIM))
            outs["cp"].append(h.reshape(bp, sp_len, cw)[:, sp_len - 2:])
            st = state_conv[e]
            zero = jnp.zeros((bs, 1, cw), F32)
            p1 = jnp.concatenate([st[:, 1:2]] + [zero] * (ls - 1), axis=1).reshape(Ts, cw)
            p2 = jnp.concatenate([st[:, 0:1], st[:, 1:2]] + [zero] * (ls - 2), axis=1).reshape(Ts, cw)
            yc, h, q, k, v, kb, vb = proj_even(xs, w_in_bf, conv_w[e], (p1, p2), ls, Ts, False)
            att = sb_decode(q.reshape(bs, ls, aw), kb.reshape(bs, ls, aw), vb.reshape(bs, ls, aw),
                            cache_kt, cache_vt, page_table + e * cache_k.shape[1], sb_bias[e], n_heads,
                            DECODE_PAGES, DECODE_TK).reshape(Ts, aw)
            xs = out_even(xs, yc, att, w_out_bf, ln_mix_g[layer], ln_mix_b[layer], alpha, Ts)
            outs["ks"].append(k.reshape(bs, ls, n_heads, HEAD_DIM))
            outs["vs"].append(v.reshape(bs, ls, n_heads, HEAD_DIM))
            hs = jnp.concatenate([st, h.reshape(bs, ls, cw)], axis=1)
            outs["cs"].append(hs[:, -2:])
        else:
            o = layer // 2
            w_in_bf = w_in_c[o].astype(BF16)
            w_grp_bf = w_grp_c[o].astype(BF16)
            w_out_bf = w_out_c[o].astype(BF16)
            buf = POOL_PAD - 1
            xp, u_tail = odd_prompt(xp, w_in_bf, w_grp_bf, scale_c[o], w_out_bf,
                                    ln_mix_g[layer], ln_mix_b[layer], alpha, TM_PROMPT)
            outs["pp"].append(u_tail[None, -buf:])
            u = matmul(xs, w_in_bf, Ts)
            ext = jnp.concatenate([jnp.zeros((bs, 1, D), F32), state_pool[o], u.reshape(bs, ls, D)], axis=1)
            d = pool_sample(ext, ls, past_len).reshape(Ts, D)
            xs = out_odd(xs, d, w_grp_bf, scale_c[o], w_out_bf,
                                ln_mix_g[layer], ln_mix_b[layer], alpha, Ts)
            outs["ps"].append(ext[:, -buf:])
        xp, xs = _ffn([xp, xs], w_router, router_bias, w_up_bf, w_down_bf, layer,
                      ln_ffn_g[layer], ln_ffn_b[layer], alpha, [TM_PROMPT, Ts], MOE_TILE)

    st = lambda n: jnp.stack(outs[n])
    return (xp.reshape(bp, sp_len, D), xs.reshape(bs, ls, D),
            st("kp"), st("vp"), st("ks"), st("vs"), st("cp"), st("cs"), st("pp"), st("ps"))
```
